```python
import math
import jax, jax.numpy as jnp
from jax import lax
import numpy as np

D_MODEL = 4096
BATCH = 32
SEQ = 256
DEPTH = 2
DEC_BATCH = 2
DEC_SEQ = 2048
PAST_LEN = 512

GRID_W = 64
D_MIX = D_MODEL
HEAD_DIM = 128
EPS = 1e-6
ATTN_BLOCK = 128
HY_W = 1024
HY_ORDER = 2
HY_SHORT = 3
HY_BANDS = 16
HY_POS_DIM = 1 + 2 * HY_BANDS
HY_FILT_HID = 64
HY_TARGET = 1e-2
HY_FAST_DECAY = 0.3
HY_SLOW_DECAY = 1.5
NA_HEADS = 8
NA_W = NA_HEADS * HEAD_DIM
NA_WR = 8
NA_WC = 16
MLA_HEADS = 8
MLA_NOPE = 128
MLA_ROPE = 64
MLA_V = 128
MLA_Q_LORA = 768
MLA_KV_LORA = 512
MLA_IN = MLA_Q_LORA + MLA_KV_LORA + MLA_ROPE
MLA_SCALE = (MLA_NOPE + MLA_ROPE) ** -0.5
ROPE_BASE = 10000.0
RET_HEADS = 8
RET_DK = 128
RET_DV = 128
RET_W = RET_HEADS * RET_DV
RET_QK = RET_HEADS * RET_DK
RET_IN = 2 * RET_QK + 2 * RET_W
RET_CHUNK = 128
N_IN = 3 * HY_W + 3 * NA_W + MLA_IN + RET_IN
N_EXPERTS = 32
TOP_K = 4
D_FF = 2048
SWIGLU_LIMIT = 7.0
SWIGLU_ALPHA = 1.702
MOE_BLOCK = 128

kernel_name = 'hymba_style_diffusion_hybrid_step'


def rms_norm(x, g):
    xf = x.astype(jnp.float32)
    y = xf * lax.rsqrt(jnp.mean(xf * xf, axis=-1, keepdims=True) + EPS)
    return (y * g.astype(jnp.float32)).astype(x.dtype)


def to_heads(x, n_heads):
    b, l, _ = x.shape
    return x.reshape(b, l, n_heads, -1).transpose(0, 2, 1, 3)


def from_heads(x):
    b, h, l, d = x.shape
    return x.transpose(0, 2, 1, 3).reshape(b, l, h * d)


def modulation(cvec, lw):
    m = jax.nn.silu(cvec) @ lw['w_mod'] + lw['b_mod']
    return jnp.split(m, 6, axis=-1)


def split_projection(h, w_in):
    sizes = (3 * HY_W, NA_W, NA_W, NA_W, MLA_IN, RET_IN)
    cuts = [int(v) for v in np.cumsum(sizes)[:-1]]
    return jnp.split(h @ w_in, cuts, axis=-1)


def blocked_attention(q, k, v, scale):
    b, h, lq, dq = q.shape
    nb = lq // ATTN_BLOCK
    qb = jnp.moveaxis(q.reshape(b, h, nb, ATTN_BLOCK, dq), 2, 0)

    def one_block(qi):
        s = jnp.einsum('bhqd,bhkd->bhqk', qi, k).astype(jnp.float32) * scale
        p = jax.nn.softmax(s, axis=-1).astype(v.dtype)
        return jnp.einsum('bhqk,bhkd->bhqd', p, v)

    o = lax.map(one_block, qb)
    return jnp.moveaxis(o, 0, 2).reshape(b, h, lq, v.shape[-1])


def short_conv(u, w, b):
    l = u.shape[1]
    up = jnp.pad(u, ((0, 0), (1, 1), (0, 0)))
    return up[:, :l] * w[0] + up[:, 1:l + 1] * w[1] + up[:, 2:] * w[2] + b


def hyena_filters(l, w1, b1, w2, b2, w3):
    pos = jnp.arange(l, dtype=jnp.float32)
    t = pos / l
    bands = jnp.linspace(1e-4, HY_BANDS - 1, HY_BANDS, dtype=jnp.float32)
    ang = (2.0 * math.pi * t)[:, None] * bands[None, :]
    z = jnp.concatenate([t[:, None], jnp.cos(ang), -jnp.sin(ang)], axis=-1)
    h = jnp.sin(z @ w1.astype(jnp.float32) + b1.astype(jnp.float32))
    h = jnp.sin(h @ w2.astype(jnp.float32) + b2.astype(jnp.float32))
    h = h @ w3.astype(jnp.float32)
    deltas = jnp.abs(jnp.linspace(math.log(HY_TARGET) / HY_FAST_DECAY,
                                  math.log(HY_TARGET) / HY_SLOW_DECAY, HY_W, dtype=jnp.float32))
    window = jnp.exp(-t[:, None] * deltas[None, :])
    return h.reshape(l, HY_ORDER, 2, HY_W) * window[:, None, None, :]


def long_conv(u, h_fwd, h_bwd, skip):
    l = u.shape[1]
    h_full = jnp.concatenate([h_fwd, jnp.zeros_like(h_fwd[:1]), h_bwd[:0:-1]], axis=0)
    uf = jnp.fft.rfft(u, n=2 * l, axis=1)
    hf = jnp.fft.rfft(h_full, axis=0)
    y = jnp.fft.irfft(uf * hf[None], n=2 * l, axis=1)[:, :l]
    return y + u * skip


def hyena_mixer(u_in, lw):
    l = u_in.shape[1]
    u = short_conv(u_in, lw['hy_conv_w'], lw['hy_conv_b']).astype(jnp.float32)
    x1, x2, v = jnp.split(u, 3, axis=-1)
    filt = hyena_filters(l, lw['hy_filt_w1'], lw['hy_filt_b1'], lw['hy_filt_w2'],
                         lw['hy_filt_b2'], lw['hy_filt_w3'])
    skip = lw['hy_bias'].astype(jnp.float32)
    z = v
    for o, gate in enumerate((x1, x2)):
        z = gate * long_conv(z, filt[:, o, 0], filt[:, o, 1], skip[o])
    return z.astype(u_in.dtype)


def natten_latent(q, k, v, k_ctx, v_ctx, rpb):
    b, h, l, d = q.shape
    rows = l // GRID_W
    wr = min(NA_WR, rows)
    r = jnp.arange(rows)
    row_start = jnp.clip(r - wr // 2, 0, rows - wr)
    key_rows = row_start[:, None] + jnp.arange(wr)[None, :]
    cidx = jnp.arange(GRID_W)
    col_start = jnp.clip(cidx - NA_WC // 2, 0, GRID_W - NA_WC)
    col_in = (cidx[None, :] >= col_start[:, None]) & (cidx[None, :] < col_start[:, None] + NA_WC)
    row_off = key_rows - r[:, None] + (NA_WR - 1)
    col_off = jnp.clip(cidx[None, :] - cidx[:, None], -(NA_WC - 1), NA_WC - 1) + (NA_WC - 1)
    bias = rpb[:, row_off[:, None, :, None], col_off[None, :, None, :]].astype(jnp.float32)
    scale = d ** -0.5
    qr = q.reshape(b, h, rows, GRID_W, d)
    kr = k.reshape(b, h, rows, GRID_W, d)[:, :, key_rows]
    vr = v.reshape(b, h, rows, GRID_W, d)[:, :, key_rows]
    s_loc = jnp.einsum('bhrqd,bhrwkd->bhrqwk', qr, kr).astype(jnp.float32) * scale + bias[None]
    s_loc = jnp.where(col_in[:, None, :], s_loc, -jnp.inf).reshape(b, h, rows, GRID_W, wr * GRID_W)
    s_ctx = jnp.einsum('bhrqd,bhcd->bhrqc', qr, k_ctx).astype(jnp.float32) * scale
    p = jax.nn.softmax(jnp.concatenate([s_loc, s_ctx], axis=-1), axis=-1)
    n_loc = wr * GRID_W
    p_loc = p[..., :n_loc].reshape(b, h, rows, GRID_W, wr, GRID_W).astype(v.dtype)
    o = (jnp.einsum('bhrqwk,bhrwkd->bhrqd', p_loc, vr)
         + jnp.einsum('bhrqc,bhcd->bhrqd', p[..., n_loc:].astype(v.dtype), v_ctx))
    return o.reshape(b, h, l, d)


def axial_rope(x):
    l = x.shape[-2]
    pos = jnp.arange(l)
    half = MLA_ROPE // 2
    freqs = jnp.power(ROPE_BASE, -jnp.arange(0, half, 2, dtype=jnp.float32) / half)

    def rot(xa, p):
        ang = p.astype(jnp.float32)[:, None] * freqs[None, :]
        cos, sin = jnp.cos(ang), jnp.sin(ang)
        a1, a2 = jnp.split(xa.astype(jnp.float32), 2, axis=-1)
        return jnp.concatenate([a1 * cos - a2 * sin, a2 * cos + a1 * sin], axis=-1)

    out = jnp.concatenate([rot(x[..., :half], pos // GRID_W), rot(x[..., half:], pos % GRID_W)], axis=-1)
    return out.astype(x.dtype)


def mla_project(m_in, lw):
    q_c, ckv, krope = jnp.split(m_in, [MLA_Q_LORA, MLA_Q_LORA + MLA_KV_LORA], axis=-1)
    q = to_heads(rms_norm(q_c, lw['mla_q_norm']) @ lw['mla_w_uq'], MLA_HEADS)
    ckv = rms_norm(ckv, lw['mla_kv_norm'])
    return q[..., :MLA_NOPE], q[..., MLA_NOPE:], ckv, krope


def mla_expand(ckv, lw):
    kv = to_heads(ckv @ lw['mla_w_ukv'], MLA_HEADS)
    return kv[..., :MLA_NOPE], kv[..., MLA_NOPE:]


def mla_keys(k_nope, krope):
    kr = jnp.broadcast_to(krope[:, None], k_nope.shape[:3] + (MLA_ROPE,))
    return jnp.concatenate([k_nope, kr], axis=-1)


def retention_chunked(q, k, v, lg, s0):
    b, h, l, dk = q.shape
    dv = v.shape[-1]
    c = RET_CHUNK
    n = l // c
    qc = q.reshape(b, h, n, c, dk)
    kc = k.reshape(b, h, n, c, dk)
    vc = v.reshape(b, h, n, c, dv)
    i = jnp.arange(c, dtype=jnp.float32)
    diff = i[:, None] - i[None, :]
    mask = (diff >= 0)[None]
    decay_in = jnp.where(mask, jnp.exp(jnp.where(mask, diff[None] * lg[:, None, None], 0.0)), 0.0)
    scores = jnp.einsum('bhncd,bhnsd->bhncs', qc, kc) * decay_in[None, :, None]
    inner = jnp.einsum('bhncs,bhnse->bhnce', scores, vc)
    k_w = jnp.exp((c - 1 - i)[None, :] * lg[:, None])
    kv_chunks = jnp.einsum('bhnsd,bhnse->bhnde', kc * k_w[None, :, None, :, None], vc)
    decay_c = jnp.exp(c * lg)[None, :, None, None]

    def step(s, kv):
        return decay_c * s + kv, s

    s_final, s_before = lax.scan(step, s0, jnp.moveaxis(kv_chunks, 2, 0))
    s_before = jnp.moveaxis(s_before, 0, 2)
    q_w = jnp.exp((i + 1)[None, :] * lg[:, None])
    cross = jnp.einsum('bhncd,bhnde->bhnce', qc * q_w[None, :, None, :, None], s_before)
    return (inner + cross).reshape(b, h, l, dv), s_final


def retention_mixer(r_in, lw, s0):
    q, k, v, g = jnp.split(r_in, [RET_QK, 2 * RET_QK, 2 * RET_QK + RET_W], axis=-1)
    q = to_heads(q, RET_HEADS).astype(jnp.float32)
    k = to_heads(k, RET_HEADS).astype(jnp.float32) * RET_DK ** -0.5
    v = to_heads(v, RET_HEADS).astype(jnp.float32)
    lg = jax.nn.log_sigmoid(lw['ret_decay'].astype(jnp.float32))
    o_f, s_f = retention_chunked(q, k, v, lg[0], s0[:, 0])
    o_b, s_b = retention_chunked(jnp.flip(q, 2), jnp.flip(k, 2), jnp.flip(v, 2), lg[1], s0[:, 1])
    o = o_f + jnp.flip(o_b, 2)
    o = o * lax.rsqrt(jnp.mean(o * o, axis=-1, keepdims=True) + EPS)
    o = (from_heads(o) * lw['ret_gn'].astype(jnp.float32)).astype(r_in.dtype)
    return o * jax.nn.silu(g), jnp.stack([s_f, s_b], axis=1)


def clamped_swiglu(hh):
    x_glu = jnp.minimum(hh[..., ::2], SWIGLU_LIMIT)
    x_lin = jnp.clip(hh[..., 1::2], -SWIGLU_LIMIT, SWIGLU_LIMIT)
    return x_glu * jax.nn.sigmoid(SWIGLU_ALPHA * x_glu) * (x_lin + 1.0)


def moe_ffn(x, lw):
    b, l, d = x.shape
    n = b * l
    xf = x.reshape(n, d)
    w1, b1, w2, b2 = lw['moe_w1'], lw['moe_b1'], lw['moe_w2'], lw['moe_b2']
    logits = (xf @ lw['router_w'] + lw['router_b']).astype(jnp.float32)
    top_val, top_idx = lax.top_k(logits, TOP_K)
    gates = jax.nn.softmax(top_val, axis=-1)
    n_assign = n * TOP_K
    flat_e = top_idx.reshape(-1)
    flat_tok = jnp.repeat(jnp.arange(n, dtype=jnp.int32), TOP_K)
    flat_g = gates.reshape(-1)
    order = jnp.argsort(flat_e)
    e_sorted, tok_sorted, g_sorted = flat_e[order], flat_tok[order], flat_g[order]
    counts = jnp.bincount(flat_e, length=N_EXPERTS)
    padded = (counts + MOE_BLOCK - 1) // MOE_BLOCK * MOE_BLOCK
    pad_end = jnp.cumsum(padded)
    pad_start = pad_end - padded
    start = jnp.cumsum(counts) - counts
    dest = pad_start[e_sorted] + jnp.arange(n_assign) - start[e_sorted]
    n_rows = n_assign + N_EXPERTS * MOE_BLOCK
    n_blocks = n_rows // MOE_BLOCK
    row_tok = jnp.full((n_rows,), n, jnp.int32).at[dest].set(tok_sorted)
    blk_e = jnp.minimum(jnp.searchsorted(pad_end, jnp.arange(n_blocks) * MOE_BLOCK, side='right'),
                        N_EXPERTS - 1)
    x_pad = jnp.concatenate([xf, jnp.zeros((1, d), xf.dtype)], axis=0)

    def expert_block(args):
        toks, e = args
        hh = x_pad[toks] @ w1[e] + b1[e]
        return clamped_swiglu(hh) @ w2[e] + b2[e]

    y_rows = lax.map(expert_block, (row_tok.reshape(n_blocks, MOE_BLOCK), blk_e)).reshape(n_rows, d)
    y_assign = y_rows[dest] * g_sorted[:, None].astype(x.dtype)
    return jax.ops.segment_sum(y_assign, tok_sorted, num_segments=n).reshape(b, l, d)


def token_mixers_context(h, lw):
    bsz = h.shape[0]
    hy, na_q, na_k, na_v, mla_in, ret_in = split_projection(h, lw['w_in'])
    y_hy = hyena_mixer(hy, lw)
    qn, kn, vn = to_heads(na_q, NA_HEADS), to_heads(na_k, NA_HEADS), to_heads(na_v, NA_HEADS)
    y_na = from_heads(blocked_attention(qn, kn, vn, HEAD_DIM ** -0.5))
    q_nope, q_rope, ckv, krope = mla_project(mla_in, lw)
    k_nope, v_m = mla_expand(ckv, lw)
    q_m = jnp.concatenate([q_nope, q_rope], axis=-1)
    y_mla = from_heads(blocked_attention(q_m, mla_keys(k_nope, krope), v_m, MLA_SCALE))
    s0 = jnp.zeros((bsz, 2, RET_HEADS, RET_DK, RET_DV), jnp.float32)
    y_ret, s_ret = retention_mixer(ret_in, lw, s0)
    y = jnp.concatenate([y_hy, y_na, y_mla, y_ret], axis=-1) @ lw['w_out']
    return y, (kn, vn, ckv, krope, s_ret.astype(h.dtype))


def token_mixers_latent(h, lw, na_k_ctx, na_v_ctx, ckv_ctx, krope_ctx, s_ret_ctx):
    hy, na_q, na_k, na_v, mla_in, ret_in = split_projection(h, lw['w_in'])
    y_hy = hyena_mixer(hy, lw)
    y_na = from_heads(natten_latent(to_heads(na_q, NA_HEADS), to_heads(na_k, NA_HEADS),
                                    to_heads(na_v, NA_HEADS), na_k_ctx, na_v_ctx, lw['na_rpb']))
    q_nope, q_rope, ckv, krope = mla_project(mla_in, lw)
    q_m = jnp.concatenate([q_nope, axial_rope(q_rope)], axis=-1)
    k_nope_l, v_l = mla_expand(ckv, lw)
    k_nope_c, v_c = mla_expand(ckv_ctx, lw)
    k_all = jnp.concatenate([mla_keys(k_nope_c, krope_ctx), mla_keys(k_nope_l, axial_rope(krope))], axis=2)
    v_all = jnp.concatenate([v_c, v_l], axis=2)
    y_mla = from_heads(blocked_attention(q_m, k_all, v_all, MLA_SCALE))
    y_ret, _ = retention_mixer(ret_in, lw, s_ret_ctx.astype(jnp.float32))
    return jnp.concatenate([y_hy, y_na, y_mla, y_ret], axis=-1) @ lw['w_out']


def context_layer(x, c_ctx, lw):
    sa, ca, ga, sf, cf, gf = modulation(c_ctx[None, None], lw)
    h = rms_norm(x, lw['norm_mix_pre']) * (1 + ca) + sa
    y, cache = token_mixers_context(h, lw)
    x = x + ga * rms_norm(y, lw['norm_mix_post'])
    h = rms_norm(x, lw['norm_ffn_pre']) * (1 + cf) + sf
    x = x + gf * rms_norm(moe_ffn(h, lw), lw['norm_ffn_post'])
    return x, cache


def latent_layer(x, c, lw, na_k_ctx, na_v_ctx, ckv_ctx, krope_ctx, s_ret_ctx):
    sa, ca, ga, sf, cf, gf = modulation(c[:, None], lw)
    h = rms_norm(x, lw['norm_mix_pre']) * (1 + ca) + sa
    y = token_mixers_latent(h, lw, na_k_ctx, na_v_ctx, ckv_ctx, krope_ctx, s_ret_ctx)
    x = x + ga * rms_norm(y, lw['norm_mix_post'])
    h = rms_norm(x, lw['norm_ffn_pre']) * (1 + cf) + sf
    return x + gf * rms_norm(moe_ffn(h, lw), lw['norm_ffn_post'])


def setup_inputs(seed: int = 0) -> dict:
    key = jax.random.key(seed)
    ks = iter(jax.random.split(key, 64))

    def nrm(shape, scale=1.0):
        return jax.random.normal(next(ks), shape, jnp.float32) * scale

    def gain(shape):
        return 1.0 + nrm(shape, 0.01)

    d = D_MODEL
    ret_base = jnp.asarray(np.log(2.0 ** (5.0 + np.arange(RET_HEADS)) - 1.0).astype(np.float32))
    return {
        'x_prompt': nrm((BATCH, SEQ, d)),
        'x_sample': nrm((DEC_BATCH, DEC_SEQ, d)),
        'cache_na_k': nrm((DEC_BATCH, DEPTH, NA_HEADS, PAST_LEN, HEAD_DIM)),
        'cache_na_v': nrm((DEC_BATCH, DEPTH, NA_HEADS, PAST_LEN, HEAD_DIM)),
        'cache_mla_ckv': nrm((DEC_BATCH, DEPTH, PAST_LEN, MLA_KV_LORA)),
        'cache_mla_krope': nrm((DEC_BATCH, DEPTH, PAST_LEN, MLA_ROPE)),
        'state_ret': nrm((DEC_BATCH, DEPTH, 2, RET_HEADS, RET_DK, RET_DV), 0.5),
        'c': nrm((DEC_BATCH, d)),
        'c_ctx': nrm((d,)),
        'w_mod': nrm((DEPTH, d, 6 * d), 0.5 * d ** -0.5),
        'b_mod': nrm((DEPTH, 6 * d), 0.02),
        'norm_mix_pre': gain((DEPTH, d)),
        'norm_mix_post': gain((DEPTH, d)),
        'norm_ffn_pre': gain((DEPTH, d)),
        'norm_ffn_post': gain((DEPTH, d)),
        'w_in': nrm((DEPTH, d, N_IN), d ** -0.5),
        'hy_conv_w': nrm((DEPTH, HY_SHORT, 3 * HY_W), HY_SHORT ** -0.5),
        'hy_conv_b': nrm((DEPTH, 3 * HY_W), 0.02),
        'hy_filt_w1': nrm((DEPTH, HY_POS_DIM, HY_FILT_HID), HY_POS_DIM ** -0.5),
        'hy_filt_b1': nrm((DEPTH, HY_FILT_HID), 0.1),
        'hy_filt_w2': nrm((DEPTH, HY_FILT_HID, HY_FILT_HID), HY_FILT_HID ** -0.5),
        'hy_filt_b2': nrm((DEPTH, HY_FILT_HID), 0.1),
        'hy_filt_w3': nrm((DEPTH, HY_FILT_HID, HY_ORDER * 2 * HY_W), 0.1 * HY_FILT_HID ** -0.5),
        'hy_bias': nrm((DEPTH, HY_ORDER, HY_W), 0.1),
        'na_rpb': nrm((DEPTH, NA_HEADS, 2 * NA_WR - 1, 2 * NA_WC - 1), 0.02),
        'mla_q_norm': gain((DEPTH, MLA_Q_LORA)),
        'mla_w_uq': nrm((DEPTH, MLA_Q_LORA, MLA_HEADS * (MLA_NOPE + MLA_ROPE)), MLA_Q_LORA ** -0.5),
        'mla_kv_norm': gain((DEPTH, MLA_KV_LORA)),
        'mla_w_ukv': nrm((DEPTH, MLA_KV_LORA, MLA_HEADS * (MLA_NOPE + MLA_V)), MLA_KV_LORA ** -0.5),
        'ret_decay': ret_base[None, None, :] + nrm((DEPTH, 2, RET_HEADS), 0.01),
        'ret_gn': gain((DEPTH, RET_W)),
        'w_out': nrm((DEPTH, D_MIX, d), D_MIX ** -0.5),
        'router_w': nrm((DEPTH, d, N_EXPERTS), d ** -0.5),
        'router_b': nrm((DEPTH, N_EXPERTS), 0.01),
        'moe_w1': nrm((DEPTH, N_EXPERTS, d, 2 * D_FF), d ** -0.5),
        'moe_b1': nrm((DEPTH, N_EXPERTS, 2 * D_FF), 0.01),
        'moe_w2': nrm((DEPTH, N_EXPERTS, D_FF, d), D_FF ** -0.5),
        'moe_b2': nrm((DEPTH, N_EXPERTS, d), 0.01),
    }


def reference(x_prompt, x_sample, cache_na_k, cache_na_v, cache_mla_ckv, cache_mla_krope, state_ret,
              c, c_ctx, w_mod, b_mod, norm_mix_pre, norm_mix_post, norm_ffn_pre, norm_ffn_post,
              w_in, hy_conv_w, hy_conv_b, hy_filt_w1, hy_filt_b1, hy_filt_w2, hy_filt_b2, hy_filt_w3,
              hy_bias, na_rpb, mla_q_norm, mla_w_uq, mla_kv_norm, mla_w_ukv, ret_decay, ret_gn,
              w_out, router_w, router_b, moe_w1, moe_b1, moe_w2, moe_b2):
    xp, xs = x_prompt, x_sample
    new_k, new_v, new_ckv, new_kr, new_s = [], [], [], [], []
    for l in range(DEPTH):
        lw = {
            'w_mod': w_mod[l], 'b_mod': b_mod[l],
            'norm_mix_pre': norm_mix_pre[l], 'norm_mix_post': norm_mix_post[l],
            'norm_ffn_pre': norm_ffn_pre[l], 'norm_ffn_post': norm_ffn_post[l],
            'w_in': w_in[l], 'hy_conv_w': hy_conv_w[l], 'hy_conv_b': hy_conv_b[l],
            'hy_filt_w1': hy_filt_w1[l], 'hy_filt_b1': hy_filt_b1[l],
            'hy_filt_w2': hy_filt_w2[l], 'hy_filt_b2': hy_filt_b2[l],
            'hy_filt_w3': hy_filt_w3[l], 'hy_bias': hy_bias[l], 'na_rpb': na_rpb[l],
            'mla_q_norm': mla_q_norm[l], 'mla_w_uq': mla_w_uq[l],
            'mla_kv_norm': mla_kv_norm[l], 'mla_w_ukv': mla_w_ukv[l],
            'ret_decay': ret_decay[l], 'ret_gn': ret_gn[l], 'w_out': w_out[l],
            'router_w': router_w[l], 'router_b': router_b[l],
            'moe_w1': moe_w1[l], 'moe_b1': moe_b1[l], 'moe_w2': moe_w2[l], 'moe_b2': moe_b2[l],
        }
        xp, (k_c, v_c, ckv_c, kr_c, s_c) = context_layer(xp, c_ctx, lw)
        new_k.append(k_c)
        new_v.append(v_c)
        new_ckv.append(ckv_c)
        new_kr.append(kr_c)
        new_s.append(s_c)
        xs = latent_layer(xs, c, lw, cache_na_k[:, l], cache_na_v[:, l], cache_mla_ckv[:, l],
                          cache_mla_krope[:, l], state_ret[:, l])
    return (xp, xs, jnp.stack(new_k, axis=1), jnp.stack(new_v, axis=1), jnp.stack(new_ckv, axis=1),
            jnp.stack(new_kr, axis=1), jnp.stack(new_s, axis=1))
```

```python
import functools
import math

import jax
import jax.numpy as jnp
from jax import lax
from jax.experimental import pallas as pl
from jax.experimental.pallas import tpu as pltpu

F32 = jnp.float32
BF16 = jnp.bfloat16
SDS = jax.ShapeDtypeStruct

EPS = 1e-6
GRID_W = 64
HEAD_DIM = 128
N_HEADS = 8
GROUP_W = N_HEADS * HEAD_DIM
HY_BANDS = 16
HY_TARGET = 1e-2
HY_FAST_DECAY = 0.3
HY_SLOW_DECAY = 1.5
NA_WR = 8
NA_WC = 16
MLA_Q_LORA = 768
MLA_KV_LORA = 512
MLA_ROPE = 64
MLA_NOPE = 128
MLA_SCALE = (MLA_NOPE + MLA_ROPE) ** -0.5
ROPE_BASE = 10000.0
TOP_K = 4
SWIGLU_LIMIT = 7.0
SWIGLU_ALPHA = 1.702
LANES = 128

COL_QC = 0
COL_KROPE = 768
COL_CKV = 1024
COL_HY = 2048
COL_NA = 5120
COL_RET = 8192
N_PROJ = 12288

VMEM_LIMIT_MB = 56


def _cp(sem, vmem_mb=48):
    return pltpu.CompilerParams(dimension_semantics=sem, vmem_limit_bytes=min(vmem_mb, VMEM_LIMIT_MB) * 2**20)


def _dot(a, b):
    return jnp.dot(a, b, preferred_element_type=F32)


def _dot_nt(a, b):
    return lax.dot_general(a, b, (((1,), (1,)), ((), ())), preferred_element_type=F32)


def _dot_tn(a, b):
    return lax.dot_general(a, b, (((0,), (0,)), ((), ())), preferred_element_type=F32)


def _silu(x):
    return x * jax.nn.sigmoid(x)


def _mod_body(c_ref, w_ref, b_ref, o_ref):
    s = _silu(c_ref[...])
    o_ref[...] = _dot(s.astype(BF16), w_ref[...].astype(BF16)) + b_ref[...]


def _modulation(cv, w_mod, b_mod, layer):
    _, d, n = w_mod.shape
    tn = 512
    return pl.pallas_call(
        _mod_body,
        grid=(n // tn,),
        in_specs=[pl.BlockSpec((8, d), lambda j: (0, 0)),
                  pl.BlockSpec((None, d, tn), lambda j: (layer, 0, j)),
                  pl.BlockSpec((None, 1, tn), lambda j: (layer, 0, j))],
        out_specs=pl.BlockSpec((8, tn), lambda j: (0, j)),
        out_shape=SDS((8, n), F32),
        compiler_params=_cp(("arbitrary",), 40),
        name="modulation",
    )(cv, w_mod, b_mod.reshape(b_mod.shape[0], 1, n))


def _norm_matmul_body(*refs, do_norm, has_mod, emit_h):
    it = iter(refs)
    x_ref = next(it)
    g_ref = next(it) if do_norm else None
    sc_ref = next(it) if has_mod else None
    sh_ref = next(it) if has_mod else None
    w_ref = next(it)
    o_ref = next(it)
    h_ref = next(it) if emit_h else None
    hs_ref = next(it)

    @pl.when(pl.program_id(1) == 0)
    def _():
        h = x_ref[...].astype(F32)
        if do_norm:
            h = h * lax.rsqrt(jnp.mean(h * h, axis=-1, keepdims=True) + EPS) * g_ref[...]
        if has_mod:
            h = h * (1.0 + sc_ref[...]) + sh_ref[...]
        if emit_h:
            h_ref[...] = h
        hs_ref[...] = h.astype(BF16)

    o_ref[...] = _dot(hs_ref[...], w_ref[...].astype(BF16)).astype(o_ref.dtype)


def _norm_matmul(x, xcol, k, w, *, tm, tn, gain=None, mod=None, grp=None, emit_h=False,
                 out_dtype=F32, xrow0=0, m=None, name="norm_matmul"):
    m = x.shape[0] if m is None else m
    n = w.shape[1]
    do_norm = gain is not None
    has_mod = mod is not None
    in_specs = [pl.BlockSpec((tm, k), lambda i, j: (i + xrow0, xcol))]
    args = [x]
    if do_norm:
        in_specs.append(pl.BlockSpec((1, k), lambda i, j: (0, 0)))
        args.append(gain.reshape(1, k))
    if has_mod:
        for a in mod:
            in_specs.append(pl.BlockSpec((None, 1, k), lambda i, j: (grp(i), 0, 0)))
            args.append(a)
    in_specs.append(pl.BlockSpec((k, tn), lambda i, j: (0, j)))
    args.append(w)
    out_specs = [pl.BlockSpec((tm, tn), lambda i, j: (i, j))]
    out_shape = [SDS((m, n), out_dtype)]
    if emit_h:
        out_specs.append(pl.BlockSpec((tm, k), lambda i, j: (i, 0)))
        out_shape.append(SDS((m, k), F32))
    res = pl.pallas_call(
        functools.partial(_norm_matmul_body, do_norm=do_norm, has_mod=has_mod, emit_h=emit_h),
        grid=(m // tm, n // tn),
        in_specs=in_specs, out_specs=out_specs, out_shape=out_shape,
        scratch_shapes=[pltpu.VMEM((tm, k), BF16)],
        compiler_params=_cp(("arbitrary", "arbitrary"), 48),
        name=name,
    )(*args)
    return res if emit_h else res[0]


def _dft_tables(l):
    n = 2 * l
    f = jnp.arange(l, dtype=jnp.int32)
    ang = ((f[:, None] * f[None, :]) % n).astype(F32) * (2.0 * math.pi / n)
    fc = jnp.cos(ang)
    alt = jnp.where(f % 2 == 0, 1.0, -1.0).astype(F32)
    fs = jnp.where(f[:, None] == 0, alt[None, :], jnp.sin(ang))
    return fc.astype(BF16), fs.astype(BF16), fs.T.astype(BF16)


def _hy_spec_body(fc_ref, fs_ref, hf_ref, hb_ref, a_ref, b_ref, *, l):
    hf = hf_ref[...]
    row = lax.broadcasted_iota(jnp.int32, hf.shape, 0)
    hb = jnp.where(row == 0, 0.0, hb_ref[...])
    a = _dot(fc_ref[...], (hf + hb).astype(BF16))
    p1 = _dot(fs_ref[...], hf.astype(BF16))
    p2 = _dot(fs_ref[...], hb.astype(BF16))
    orow = lax.broadcasted_iota(jnp.int32, a.shape, 0)
    first = jnp.logical_and(pl.program_id(0) == 0, orow == 0)
    wgt = jnp.where(first, 0.5 / l, 1.0 / l)
    a_ref[...] = wgt * a
    b_ref[...] = wgt * jnp.where(first, p1 + p2, p1 - p2)


def _hy_spectra(fc, fs, hf, hb):
    l, c2 = hf.shape
    tf = min(l, 256)
    tc = 512
    return pl.pallas_call(
        functools.partial(_hy_spec_body, l=l),
        grid=(l // tf, c2 // tc),
        in_specs=[pl.BlockSpec((tf, l), lambda i, j: (i, 0)),
                  pl.BlockSpec((tf, l), lambda i, j: (i, 0)),
                  pl.BlockSpec((l, tc), lambda i, j: (0, j)),
                  pl.BlockSpec((l, tc), lambda i, j: (0, j))],
        out_specs=[pl.BlockSpec((tf, tc), lambda i, j: (i, j))] * 2,
        out_shape=[SDS((l, c2), F32)] * 2,
        compiler_params=_cp(("arbitrary", "arbitrary"), 40),
        name="hyena_filter_spectra",
    )(fc, fs, hf, hb)


def _short_conv(x, w, b):
    l = x.shape[0]
    row = lax.broadcasted_iota(jnp.int32, x.shape, 0)
    xm = jnp.where(row == 0, 0.0, pltpu.roll(x, 1, 0))
    xp = jnp.where(row == l - 1, 0.0, pltpu.roll(x, l - 1, 0))
    return xm * w[0:1, :] + x * w[1:2, :] + xp * w[2:3, :] + b


def _hy_conv_body(*refs, first_order, nf):
    it = iter(refs)
    z_ref = next(it)
    zw_ref = next(it) if first_order else None
    zb_ref = next(it) if first_order else None
    gt_ref, gw_ref, gb_ref, skip_ref = next(it), next(it), next(it), next(it)
    ah_ref, bh_ref, fc_ref, fs_ref, gc_ref, gs_ref = (next(it) for _ in range(6))
    o_ref = next(it)
    zf_scr, zb_scr, acc = next(it), next(it), next(it)
    fb = pl.program_id(2)

    @pl.when(fb == 0)
    def _():
        z = z_ref[...].astype(F32)
        if first_order:
            z = _short_conv(z, zw_ref[...], zb_ref[...])
        zf_scr[...] = z
        zb_scr[...] = z.astype(BF16)
        acc[...] = jnp.zeros_like(acc)

    zb = zb_scr[...]
    uc = _dot(fc_ref[...], zb)
    us = _dot(fs_ref[...], zb)
    ah = ah_ref[...]
    bh = bh_ref[...]
    row = lax.broadcasted_iota(jnp.int32, uc.shape, 0)
    first = jnp.logical_and(fb == 0, row == 0)
    usb = us * bh
    yc = uc * ah - jnp.where(first, 0.0, usb)
    ys = jnp.where(first, usb, uc * bh + us * ah)
    acc[...] += _dot(gc_ref[...], yc.astype(BF16)) + _dot(gs_ref[...], ys.astype(BF16))

    @pl.when(fb == nf - 1)
    def _():
        gate = _short_conv(gt_ref[...].astype(F32), gw_ref[...], gb_ref[...])
        o_ref[...] = (gate * (acc[...] + zf_scr[...] * skip_ref[...])).astype(o_ref.dtype)


def _hy_conv(z, gate_col, proj, rb, conv_w, conv_b, skip, ah, bh, spec_col0, tables, *,
             nb, l, z_col, out_dtype):
    fc, fs, gs = tables
    first_order = z_col is not None
    tc = 512 if l > 512 else GROUP_W
    tf = min(l, 256)
    nf = l // tf

    def proj_specs(col):
        return [pl.BlockSpec((l, tc), lambda b, c, f: (b + rb, col // tc + c)),
                pl.BlockSpec((3, tc), lambda b, c, f: (0, (col - COL_HY) // tc + c)),
                pl.BlockSpec((1, tc), lambda b, c, f: (0, (col - COL_HY) // tc + c))]

    if first_order:
        in_specs = proj_specs(z_col)
        args = [proj, conv_w, conv_b]
    else:
        in_specs = [pl.BlockSpec((l, tc), lambda b, c, f: (b, c))]
        args = [z]
    in_specs += proj_specs(gate_col)
    args += [proj, conv_w, conv_b]
    in_specs += [pl.BlockSpec((1, tc), lambda b, c, f: (0, c)),
                 pl.BlockSpec((tf, tc), lambda b, c, f: (f, spec_col0 // tc + c)),
                 pl.BlockSpec((tf, tc), lambda b, c, f: (f, spec_col0 // tc + c)),
                 pl.BlockSpec((tf, l), lambda b, c, f: (f, 0)),
                 pl.BlockSpec((tf, l), lambda b, c, f: (f, 0)),
                 pl.BlockSpec((l, tf), lambda b, c, f: (0, f)),
                 pl.BlockSpec((l, tf), lambda b, c, f: (0, f))]
    args += [skip, ah, bh, fc, fs, fc, gs]
    return pl.pallas_call(
        functools.partial(_hy_conv_body, first_order=first_order, nf=nf),
        grid=(nb, GROUP_W // tc, nf),
        in_specs=in_specs,
        out_specs=pl.BlockSpec((l, tc), lambda b, c, f: (b, c)),
        out_shape=SDS((nb * l, GROUP_W), out_dtype),
        scratch_shapes=[pltpu.VMEM((l, tc), F32), pltpu.VMEM((l, tc), BF16), pltpu.VMEM((l, tc), F32)],
        compiler_params=_cp(("arbitrary", "arbitrary", "arbitrary"), 52),
        name="hyena_conv",
    )(*args)


def _hyena_filters(l, w1, b1, w2, b2, w3):
    hp = lax.Precision.HIGHEST
    pos = jnp.arange(l, dtype=F32)
    t = pos / l
    bands = jnp.linspace(1e-4, HY_BANDS - 1, HY_BANDS, dtype=F32)
    ang = (2.0 * math.pi * t)[:, None] * bands[None, :]
    z = jnp.concatenate([t[:, None], jnp.cos(ang), -jnp.sin(ang)], axis=-1)
    h = jnp.sin(jnp.dot(z, w1, precision=hp) + b1)
    h = jnp.sin(jnp.dot(h, w2, precision=hp) + b2)
    h = jnp.dot(h, w3, precision=hp)
    deltas = jnp.abs(jnp.linspace(math.log(HY_TARGET) / HY_FAST_DECAY,
                                  math.log(HY_TARGET) / HY_SLOW_DECAY, GROUP_W, dtype=F32))
    window = jnp.exp(-t[:, None] * deltas[None, :])
    filt = h.reshape(l, 2, 2, GROUP_W) * window[:, None, None, :]
    return filt[:, :, 0, :].reshape(l, 2 * GROUP_W), filt[:, :, 1, :].reshape(l, 2 * GROUP_W)


def _hyena(proj, row0, nb, l, lw, tables):
    fc, fs, gs = tables
    hf, hb = _hyena_filters(l, lw["hy_filt_w1"], lw["hy_filt_b1"], lw["hy_filt_w2"], lw["hy_filt_b2"],
                            lw["hy_filt_w3"])
    ah, bh = _hy_spectra(fc, fs, hf, hb)
    cw = lw["hy_conv_w"]
    cb = lw["hy_conv_b"].reshape(1, -1)
    skip = lw["hy_bias"]
    rb = row0 // l
    z1 = _hy_conv(None, COL_HY, proj, rb, cw, cb, skip[0:1], ah, bh, 0, tables,
                  nb=nb, l=l, z_col=COL_HY + 2 * GROUP_W, out_dtype=F32)
    return _hy_conv(z1, COL_HY + GROUP_W, proj, rb, cw, cb, skip[1:2], ah, bh, GROUP_W, tables,
                    nb=nb, l=l, z_col=None, out_dtype=BF16)


def _softmax_attend(scores, values):
    m = functools.reduce(jnp.maximum, [jnp.max(s, axis=-1, keepdims=True) for s in scores])
    ps = [jnp.exp(s - m) for s in scores]
    den = functools.reduce(jnp.add, [jnp.sum(p, axis=-1, keepdims=True) for p in ps])
    o = functools.reduce(jnp.add, [_dot(p.astype(BF16), v) for p, v in zip(ps, values)])
    return o / den


def _na_ctx_body(q_ref, k_ref, v_ref, o_ref, kc_ref, vc_ref):
    scale = HEAD_DIM ** -0.5
    for h in range(N_HEADS):
        hs = slice(h * HEAD_DIM, (h + 1) * HEAD_DIM)
        k = k_ref[:, hs]
        v = v_ref[:, hs]
        kc_ref[h] = k
        vc_ref[h] = v
        s = _dot_nt(q_ref[:, hs].astype(BF16), k.astype(BF16)) * scale
        o_ref[:, hs] = _softmax_attend([s], [v.astype(BF16)]).astype(o_ref.dtype)


def _na_context(proj, nb, l):
    c0 = COL_NA // GROUP_W
    return pl.pallas_call(
        _na_ctx_body,
        grid=(nb,),
        in_specs=[pl.BlockSpec((l, GROUP_W), lambda b: (b, c0)),
                  pl.BlockSpec((l, GROUP_W), lambda b: (b, c0 + 1)),
                  pl.BlockSpec((l, GROUP_W), lambda b: (b, c0 + 2))],
        out_specs=[pl.BlockSpec((l, GROUP_W), lambda b: (b, 0)),
                   pl.BlockSpec((None, N_HEADS, l, HEAD_DIM), lambda b: (b, 0, 0, 0)),
                   pl.BlockSpec((None, N_HEADS, l, HEAD_DIM), lambda b: (b, 0, 0, 0))],
        out_shape=[SDS((nb * l, GROUP_W), BF16),
                   SDS((nb, N_HEADS, l, HEAD_DIM), F32),
                   SDS((nb, N_HEADS, l, HEAD_DIM), F32)],
        compiler_params=_cp(("arbitrary",), 32),
        name="na_context",
    )(proj, proj, proj)


def _na_bias_table(rpb):
    cidx = jnp.arange(GRID_W)
    col_start = jnp.clip(cidx - NA_WC // 2, 0, GRID_W - NA_WC)
    col_in = (cidx[None, :] >= col_start[:, None]) & (cidx[None, :] < col_start[:, None] + NA_WC)
    col_off = jnp.clip(cidx[None, :] - cidx[:, None], -(NA_WC - 1), NA_WC - 1) + (NA_WC - 1)
    t = rpb[:, :, col_off].astype(F32)
    t = jnp.where(col_in[None, None], t, -jnp.inf)
    return t.transpose(0, 2, 1, 3).reshape(rpb.shape[0], GRID_W, -1)


def _na_lat_body(q_ref, k_ref, v_ref, kc_ref, vc_ref, bt_ref, o_ref, *, rows):
    scale = HEAD_DIM ** -0.5
    wr = min(NA_WR, rows)
    kcx = kc_ref[...].astype(BF16)
    vcx = vc_ref[...].astype(BF16)
    nloc = wr * GRID_W
    bt = bt_ref[...]
    for r in range(rows):
        rs = min(max(r - wr // 2, 0), rows - wr)
        off = rs - r + (NA_WR - 1)
        q = q_ref[r * GRID_W:(r + 1) * GRID_W, :].astype(BF16)
        kl = k_ref[rs * GRID_W:rs * GRID_W + nloc, :].astype(BF16)
        vl = v_ref[rs * GRID_W:rs * GRID_W + nloc, :].astype(BF16)
        s_loc = _dot_nt(q, kl) * scale + bt[:, off * GRID_W:off * GRID_W + nloc]
        s_ctx = _dot_nt(q, kcx) * scale
        o_ref[r * GRID_W:(r + 1) * GRID_W, :] = _softmax_attend([s_loc, s_ctx], [vl, vcx]).astype(o_ref.dtype)


def _na_latent(proj, row0, nb, l, cache_k, cache_v, layer, bias_table):
    rows = l // GRID_W
    rb = row0 // l
    c0 = COL_NA // HEAD_DIM
    past = cache_k.shape[3]
    cspec = pl.BlockSpec((None, None, None, past, HEAD_DIM), lambda b, h: (b, layer, h, 0, 0))
    return pl.pallas_call(
        functools.partial(_na_lat_body, rows=rows),
        grid=(nb, N_HEADS),
        in_specs=[pl.BlockSpec((l, HEAD_DIM), lambda b, h: (b + rb, c0 + h)),
                  pl.BlockSpec((l, HEAD_DIM), lambda b, h: (b + rb, c0 + N_HEADS + h)),
                  pl.BlockSpec((l, HEAD_DIM), lambda b, h: (b + rb, c0 + 2 * N_HEADS + h)),
                  cspec, cspec,
                  pl.BlockSpec((None, GRID_W, bias_table.shape[2]), lambda b, h: (h, 0, 0))],
        out_specs=pl.BlockSpec((l, HEAD_DIM), lambda b, h: (b, h)),
        out_shape=SDS((nb * l, GROUP_W), BF16),
        compiler_params=_cp(("arbitrary", "arbitrary"), 32),
        name="na_latent",
    )(proj, proj, proj, cache_k, cache_v, bias_table)


def _mla_attn_body(*refs, nseg):
    qn_ref, qr_ref = refs[0], refs[1]
    segs = [refs[2 + 3 * i: 5 + 3 * i] for i in range(nseg)]
    o_ref = refs[2 + 3 * nseg]
    krs = [kr_ref[...][:, :MLA_ROPE].astype(BF16) for _, kr_ref, _ in segs]
    qr_all = qr_ref[...].astype(BF16)
    for h in range(N_HEADS):
        hs = slice(h * HEAD_DIM, (h + 1) * HEAD_DIM)
        qn = qn_ref[:, hs].astype(BF16)
        qr = qr_all[:, h * MLA_ROPE:(h + 1) * MLA_ROPE]
        scores = [(_dot_nt(qn, kn_ref[:, hs].astype(BF16)) + _dot_nt(qr, kr)) * MLA_SCALE
                  for (kn_ref, _, _), kr in zip(segs, krs)]
        vals = [v_ref[:, hs].astype(BF16) for _, _, v_ref in segs]
        o_ref[:, hs] = _softmax_attend(scores, vals).astype(o_ref.dtype)


def _mla_attention(qn, qr, segs, *, nb, lq, tq, name):
    arrays = [qn[0], qr[0]]
    specs = [qn[1], qr[1]]
    for s in segs:
        for a, sp in s:
            arrays.append(a)
            specs.append(sp)
    return pl.pallas_call(
        functools.partial(_mla_attn_body, nseg=len(segs)),
        grid=(nb, lq // tq),
        in_specs=specs,
        out_specs=pl.BlockSpec((tq, GROUP_W), lambda b, i: (b * (lq // tq) + i, 0)),
        out_shape=SDS((nb * lq, GROUP_W), BF16),
        compiler_params=_cp(("arbitrary", "arbitrary"), 48),
        name=name,
    )(*arrays)


def _axial_rope(x):
    l = x.shape[-2]
    pos = jnp.arange(l)
    half = MLA_ROPE // 2
    freqs = jnp.power(ROPE_BASE, -jnp.arange(0, half, 2, dtype=F32) / half)

    def rot(xa, p):
        ang = p.astype(F32)[:, None] * freqs[None, :]
        cos, sin = jnp.cos(ang), jnp.sin(ang)
        a1, a2 = jnp.split(xa, 2, axis=-1)
        return jnp.concatenate([a1 * cos - a2 * sin, a2 * cos + a1 * sin], axis=-1)

    return jnp.concatenate([rot(x[..., :half], pos // GRID_W), rot(x[..., half:], pos % GRID_W)], axis=-1)


def _ret_body(*refs, l, tq, has_state, emit_state):
    it = iter(refs)
    lg_ref = next(it)
    q_ref, k_ref, v_ref, g_ref, gn_ref = (next(it) for _ in range(5))
    s0_ref = next(it) if has_state else None
    o_ref = next(it)
    so_ref = next(it) if emit_state else None
    t0 = pl.program_id(1) * tq
    ti = lax.broadcasted_iota(jnp.int32, (tq, l), 0) + t0
    si = lax.broadcasted_iota(jnp.int32, (tq, l), 1)
    d = (ti - si).astype(F32)
    tcol = (lax.broadcasted_iota(jnp.int32, (tq, 1), 0) + t0).astype(F32)
    scol = lax.broadcasted_iota(jnp.int32, (l, 1), 0).astype(F32)
    kscale = HEAD_DIM ** -0.5
    for h in range(N_HEADS):
        hs = slice(h * HEAD_DIM, (h + 1) * HEAD_DIM)
        lgf = lg_ref[0, h]
        lgb = lg_ref[1, h]
        q = q_ref[:, hs]
        k = k_ref[:, hs] * kscale
        v = v_ref[:, hs].astype(BF16)
        s = _dot_nt(q.astype(BF16), k.astype(BF16))
        w = jnp.exp(jnp.where(d > 0, lgf, -lgb) * d)
        w = jnp.where(d == 0, 2.0, w)
        o = _dot((s * w).astype(BF16), v)
        if has_state:
            qf = q * jnp.exp(lgf * (tcol + 1.0))
            qb = q * jnp.exp(lgb * (l - tcol))
            o = o + _dot(qf.astype(BF16), s0_ref[0, h].astype(BF16)) + _dot(qb.astype(BF16), s0_ref[1, h].astype(BF16))
        o = o * lax.rsqrt(jnp.mean(o * o, axis=-1, keepdims=True) + EPS) * gn_ref[:, hs]
        o_ref[:, hs] = (o * _silu(g_ref[:, hs])).astype(o_ref.dtype)
        if emit_state:
            kf = k * jnp.exp(lgf * (l - 1.0 - scol))
            kb = k * jnp.exp(lgb * scol)
            so_ref[0, h] = _dot_tn(kf.astype(BF16), v)
            so_ref[1, h] = _dot_tn(kb.astype(BF16), v)


def _retention(proj, row0, nb, l, tq, lg, gn, state=None, layer=0, emit_state=False):
    assert not emit_state or tq == l
    nq = l // tq
    rbq = row0 // tq
    rbl = row0 // l
    c0 = COL_RET // GROUP_W
    in_specs = [pl.BlockSpec(memory_space=pltpu.SMEM),
                pl.BlockSpec((tq, GROUP_W), lambda b, i: (rbq + b * nq + i, c0)),
                pl.BlockSpec((l, GROUP_W), lambda b, i: (rbl + b, c0 + 1)),
                pl.BlockSpec((l, GROUP_W), lambda b, i: (rbl + b, c0 + 2)),
                pl.BlockSpec((tq, GROUP_W), lambda b, i: (rbq + b * nq + i, c0 + 3)),
                pl.BlockSpec((1, GROUP_W), lambda b, i: (0, 0))]
    args = [lg, proj, proj, proj, proj, gn.reshape(1, GROUP_W)]
    if state is not None:
        in_specs.append(pl.BlockSpec((None, None, 2, N_HEADS, HEAD_DIM, HEAD_DIM),
                                     lambda b, i: (b, layer, 0, 0, 0, 0)))
        args.append(state)
    out_specs = [pl.BlockSpec((tq, GROUP_W), lambda b, i: (b * nq + i, 0))]
    out_shape = [SDS((nb * l, GROUP_W), BF16)]
    if emit_state:
        out_specs.append(pl.BlockSpec((None, 2, N_HEADS, HEAD_DIM, HEAD_DIM), lambda b, i: (b, 0, 0, 0, 0)))
        out_shape.append(SDS((nb, 2, N_HEADS, HEAD_DIM, HEAD_DIM), F32))
    res = pl.pallas_call(
        functools.partial(_ret_body, l=l, tq=tq, has_state=state is not None, emit_state=emit_state),
        grid=(nb, nq),
        in_specs=in_specs, out_specs=out_specs, out_shape=out_shape,
        compiler_params=_cp(("arbitrary", "arbitrary"), 48),
        name="retention",
    )(*args)
    return res if emit_state else res[0]


def _out_proj_body(y0_ref, y1_ref, y2_ref, y3_ref, w_ref, x_ref, gpost_ref, ga_ref, gpre_ref, cf_ref, sf_ref,
                   rw_ref, rb_ref, x1_ref, h_ref, ti_ref, tg_ref, yacc, *, nj, tn):
    j = pl.program_id(1)
    acc = _dot(y0_ref[...], w_ref[0 * GROUP_W:1 * GROUP_W, :])
    acc += _dot(y1_ref[...], w_ref[1 * GROUP_W:2 * GROUP_W, :])
    acc += _dot(y2_ref[...], w_ref[2 * GROUP_W:3 * GROUP_W, :])
    acc += _dot(y3_ref[...], w_ref[3 * GROUP_W:4 * GROUP_W, :])
    yacc[j] = acc

    @pl.when(j == nj - 1)
    def _():
        d = nj * tn
        ss = functools.reduce(jnp.add, [jnp.sum(yacc[jj] * yacc[jj], axis=-1, keepdims=True) for jj in range(nj)])
        inv = lax.rsqrt(ss / d + EPS)
        ss1 = jnp.zeros_like(ss)
        for jj in range(nj):
            cs = slice(jj * tn, (jj + 1) * tn)
            x1 = x_ref[:, cs] + ga_ref[:, cs] * (yacc[jj] * inv * gpost_ref[:, cs])
            x1_ref[:, cs] = x1
            ss1 = ss1 + jnp.sum(x1 * x1, axis=-1, keepdims=True)
        inv1 = lax.rsqrt(ss1 / d + EPS)
        logits = rb_ref[...]
        for jj in range(nj):
            cs = slice(jj * tn, (jj + 1) * tn)
            hh = x1_ref[:, cs] * inv1 * gpre_ref[:, cs] * (1.0 + cf_ref[:, cs]) + sf_ref[:, cs]
            h_ref[:, cs] = hh.astype(h_ref.dtype)
            logits = logits + jnp.dot(hh, rw_ref[cs, :], precision=lax.Precision.HIGHEST,
                                      preferred_element_type=F32)
        _topk_store(logits, ti_ref, tg_ref)


def _topk_store(logits, ti_ref, tg_ref):
    lane = lax.broadcasted_iota(jnp.int32, logits.shape, 1)
    work = logits
    idx_out = jnp.zeros(logits.shape, jnp.int32)
    val_out = jnp.full(logits.shape, -jnp.inf, F32)
    for k in range(TOP_K):
        m = jnp.max(work, axis=-1, keepdims=True)
        idx = jnp.min(jnp.where(work == m, lane, LANES), axis=-1, keepdims=True)
        idx_out = jnp.where(lane == k, idx, idx_out)
        val_out = jnp.where(lane == k, m, val_out)
        work = jnp.where(lane == idx, -jnp.inf, work)
    e = jnp.exp(val_out - jnp.max(val_out, axis=-1, keepdims=True))
    ti_ref[...] = idx_out
    tg_ref[...] = e / jnp.sum(e, axis=-1, keepdims=True)


def _out_proj(ys, w_out, x, gpost, ga, gpre, cf, sf, rw, rb, grp, *, tm=256, tn=512):
    m, d = x.shape
    nj = d // tn
    yspec = pl.BlockSpec((tm, GROUP_W), lambda i, j: (i, 0))
    row = pl.BlockSpec((1, d), lambda i, j: (0, 0))
    modspec = pl.BlockSpec((None, 1, d), lambda i, j: (grp(i), 0, 0))
    full = pl.BlockSpec((tm, d), lambda i, j: (i, 0))
    small = pl.BlockSpec((tm, LANES), lambda i, j: (i, 0))
    return pl.pallas_call(
        functools.partial(_out_proj_body, nj=nj, tn=tn),
        grid=(m // tm, nj),
        in_specs=[yspec, yspec, yspec, yspec,
                  pl.BlockSpec((d, tn), lambda i, j: (0, j)),
                  full, row, modspec, row, modspec, modspec,
                  pl.BlockSpec((d, LANES), lambda i, j: (0, 0)),
                  pl.BlockSpec((1, LANES), lambda i, j: (0, 0))],
        out_specs=[full, full, small, small],
        out_shape=[SDS((m, d), F32), SDS((m, d), BF16), SDS((m, LANES), jnp.int32), SDS((m, LANES), F32)],
        scratch_shapes=[pltpu.VMEM((nj, tm, tn), F32)],
        compiler_params=_cp(("arbitrary", "arbitrary"), 52),
        name="out_proj_router",
    )(*ys, w_out, x, gpost.reshape(1, d), ga, gpre.reshape(1, d), cf, sf, rw, rb)


def _deinterleave(hh):
    tm, tn = hh.shape
    lane = lax.broadcasted_iota(jnp.int32, (tm, LANES), 1)
    half = LANES // 2
    idx = jnp.where(lane < half, 2 * lane, 2 * (lane - half) + 1)
    ev, od = [], []
    for c in range(tn // (2 * LANES)):
        g0 = jnp.take_along_axis(hh[:, (2 * c) * LANES:(2 * c + 1) * LANES], idx, axis=1)
        g1 = jnp.take_along_axis(hh[:, (2 * c + 1) * LANES:(2 * c + 2) * LANES], idx, axis=1)
        ev.append(jnp.where(lane < half, g0, pltpu.roll(g1, half, 1)))
        od.append(jnp.where(lane < half, pltpu.roll(g0, half, 1), g1))
    return jnp.concatenate(ev, axis=1), jnp.concatenate(od, axis=1)


def _cast_weights(w_ref, wb_ref, chunk=256):
    k = w_ref.shape[0]

    def step(c, carry):
        r = pl.multiple_of(c * chunk, chunk)
        wb_ref[pl.ds(r, chunk), :] = w_ref[pl.ds(r, chunk), :].astype(BF16)
        return carry

    lax.fori_loop(0, k // chunk, step, 0)


def _moe_w1_body(se_ref, sj_ref, sb_ref, sf_ref, sv_ref, x_ref, w_ref, b_ref, o_ref, wb_ref):
    s = pl.program_id(0)

    @pl.when(sv_ref[s] == 1)
    def _():
        @pl.when(sf_ref[s] == 1)
        def _():
            _cast_weights(w_ref, wb_ref)

        hh = _dot(x_ref[...], wb_ref[...]) + b_ref[...]
        glu, lin = _deinterleave(hh)
        glu = jnp.minimum(glu, SWIGLU_LIMIT)
        lin = jnp.clip(lin, -SWIGLU_LIMIT, SWIGLU_LIMIT)
        o_ref[...] = (glu * jax.nn.sigmoid(SWIGLU_ALPHA * glu) * (lin + 1.0)).astype(o_ref.dtype)


def _moe_w2_body(se_ref, sj_ref, sb_ref, sf_ref, sv_ref, a_ref, w_ref, b_ref, g_ref, o_ref, wb_ref):
    s = pl.program_id(0)

    @pl.when(sv_ref[s] == 1)
    def _():
        @pl.when(sf_ref[s] == 1)
        def _():
            _cast_weights(w_ref, wb_ref)

        o_ref[...] = (_dot(a_ref[...], wb_ref[...]) + b_ref[...]) * g_ref[...]


def _moe_schedule(counts, tm, nt, n_blocks):
    nb_e = (counts + tm - 1) // tm
    first_blk = jnp.cumsum(nb_e) - nb_e
    steps_e = nb_e * nt
    cs = jnp.cumsum(steps_e)
    total = cs[-1]
    n_steps = n_blocks * nt
    s = jnp.arange(n_steps, dtype=jnp.int32)
    sc = jnp.minimum(s, total - 1)
    e = jnp.minimum(jnp.searchsorted(cs, sc, side="right"), counts.shape[0] - 1).astype(jnp.int32)
    local = sc - (cs[e] - steps_e[e])
    nbe = jnp.maximum(nb_e[e], 1)
    j = local // nbe
    r = local % nbe
    blk = first_blk[e] + r
    valid = (s < total).astype(jnp.int32)
    first = ((r == 0) & (s < total)).astype(jnp.int32)
    return e, j.astype(jnp.int32), blk.astype(jnp.int32), first, valid


def _moe_experts(h, top_idx, top_gate, w1, b1, w2, b2, layer, *, tm=256, tn1=1024, tn2=1024):
    m, d = h.shape
    n_exp, _, ff2 = w1.shape[1:]
    ff = ff2 // 2
    n_assign = m * TOP_K
    flat_e = top_idx.reshape(-1)
    onehot = (flat_e[:, None] == jnp.arange(n_exp, dtype=jnp.int32)[None, :]).astype(jnp.int32)
    csum = jnp.cumsum(onehot, axis=0)
    counts = csum[-1]
    pos = jnp.take_along_axis(csum, flat_e[:, None], axis=1)[:, 0] - 1
    padded = (counts + tm - 1) // tm * tm
    pad_start = jnp.cumsum(padded) - padded
    dest = (pad_start[flat_e] + pos).astype(jnp.int32)
    n_rows = n_assign + n_exp * tm
    n_blocks = n_rows // tm
    tok = jnp.arange(n_assign, dtype=jnp.int32) // TOP_K
    row_tok = jnp.zeros((n_rows,), jnp.int32).at[dest].set(tok)
    row_gate = jnp.zeros((n_rows,), F32).at[dest].set(top_gate.reshape(-1))
    xs = jnp.take(h, row_tok, axis=0)

    nt1 = ff2 // tn1
    sched1 = _moe_schedule(counts, tm, nt1, n_blocks)
    act = pl.pallas_call(
        _moe_w1_body,
        grid_spec=pltpu.PrefetchScalarGridSpec(
            num_scalar_prefetch=5,
            grid=(n_blocks * nt1,),
            in_specs=[pl.BlockSpec((tm, d), lambda s, se, sj, sb, sf, sv: (sb[s], 0)),
                      pl.BlockSpec((None, None, d, tn1), lambda s, se, sj, sb, sf, sv: (layer, se[s], 0, sj[s])),
                      pl.BlockSpec((None, None, 1, tn1), lambda s, se, sj, sb, sf, sv: (layer, se[s], 0, sj[s]))],
            out_specs=pl.BlockSpec((tm, tn1 // 2), lambda s, se, sj, sb, sf, sv: (sb[s], sj[s])),
            scratch_shapes=[pltpu.VMEM((d, tn1), BF16)]),
        out_shape=SDS((n_rows, ff), BF16),
        compiler_params=_cp(("arbitrary",), 56),
        name="moe_w1",
    )(*sched1, xs, w1, b1.reshape(b1.shape[0], n_exp, 1, ff2))

    nt2 = d // tn2
    sched2 = _moe_schedule(counts, tm, nt2, n_blocks)
    y_rows = pl.pallas_call(
        _moe_w2_body,
        grid_spec=pltpu.PrefetchScalarGridSpec(
            num_scalar_prefetch=5,
            grid=(n_blocks * nt2,),
            in_specs=[pl.BlockSpec((tm, ff), lambda s, se, sj, sb, sf, sv: (sb[s], 0)),
                      pl.BlockSpec((None, None, ff, tn2), lambda s, se, sj, sb, sf, sv: (layer, se[s], 0, sj[s])),
                      pl.BlockSpec((None, None, 1, tn2), lambda s, se, sj, sb, sf, sv: (layer, se[s], 0, sj[s])),
                      pl.BlockSpec((tm, 1), lambda s, se, sj, sb, sf, sv: (sb[s], 0))],
            out_specs=pl.BlockSpec((tm, tn2), lambda s, se, sj, sb, sf, sv: (sb[s], sj[s])),
            scratch_shapes=[pltpu.VMEM((ff, tn2), BF16)]),
        out_shape=SDS((n_rows, d), F32),
        compiler_params=_cp(("arbitrary",), 48),
        name="moe_w2",
    )(*sched2, act, w2, b2.reshape(b2.shape[0], n_exp, 1, d), row_gate.reshape(n_rows, 1))
    return y_rows, dest.reshape(m, TOP_K)


def _combine_body(y0_ref, y1_ref, y2_ref, y3_ref, x_ref, g_ref, gf_ref, o_ref):
    y = y0_ref[...] + y1_ref[...] + y2_ref[...] + y3_ref[...]
    inv = lax.rsqrt(jnp.mean(y * y, axis=-1, keepdims=True) + EPS)
    o_ref[...] = x_ref[...] + gf_ref[...] * (y * inv * g_ref[...])


def _combine(ys, x, gpost, gf, grp, *, tm=128):
    m, d = x.shape
    full = pl.BlockSpec((tm, d), lambda i: (i, 0))
    return pl.pallas_call(
        _combine_body,
        grid=(m // tm,),
        in_specs=[full, full, full, full, full,
                  pl.BlockSpec((1, d), lambda i: (0, 0)),
                  pl.BlockSpec((None, 1, d), lambda i: (grp(i), 0, 0))],
        out_specs=full,
        out_shape=SDS((m, d), F32),
        compiler_params=_cp(("arbitrary",), 56),
        name="moe_combine",
    )(*ys, x, gpost.reshape(1, d), gf)


def _prep_w_in(w_in):
    k = w_in.shape[1]
    hy_na = 6 * GROUP_W
    mla0 = hy_na
    ret0 = mla0 + MLA_Q_LORA + MLA_KV_LORA + MLA_ROPE

    def z(n):
        return jnp.zeros(w_in.shape[:2] + (n,), w_in.dtype)

    parts = [w_in[..., mla0:mla0 + MLA_Q_LORA],
             w_in[..., mla0 + MLA_Q_LORA + MLA_KV_LORA:ret0], z(COL_CKV - COL_KROPE - MLA_ROPE),
             w_in[..., mla0 + MLA_Q_LORA:mla0 + MLA_Q_LORA + MLA_KV_LORA], z(COL_HY - COL_CKV - MLA_KV_LORA),
             w_in[..., :hy_na], w_in[..., ret0:]]
    return jnp.concatenate(parts, axis=-1).astype(BF16)


def _prep_heads(w, d_a, d_b):
    lead = w.shape[:-1]
    w = w.reshape(lead + (N_HEADS, d_a + d_b))
    a = w[..., :d_a].reshape(lead + (N_HEADS * d_a,))
    b = w[..., d_a:].reshape(lead + (N_HEADS * d_b,))
    return jnp.concatenate([a, b], axis=-1).astype(BF16)


def kernel(x_prompt, x_sample, cache_na_k, cache_na_v, cache_mla_ckv, cache_mla_krope, state_ret, c, c_ctx, w_mod, b_mod, norm_mix_pre, norm_mix_post, norm_ffn_pre, norm_ffn_post, w_in, hy_conv_w, hy_conv_b, hy_filt_w1, hy_filt_b1, hy_filt_w2, hy_filt_b2, hy_filt_w3, hy_bias, na_rpb, mla_q_norm, mla_w_uq, mla_kv_norm, mla_w_ukv, ret_decay, ret_gn, w_out, router_w, router_b, moe_w1, moe_b1, moe_w2, moe_b2):
    nb_c, l_c, d = x_prompt.shape
    nb_l, l_l, _ = x_sample.shape
    depth = w_in.shape[0]
    past = cache_na_k.shape[3]
    m_c = nb_c * l_c
    m_l = nb_l * l_l
    n_exp = router_w.shape[2]

    x = jnp.concatenate([x_prompt.reshape(m_c, d), x_sample.reshape(m_l, d)], axis=0)
    cv = jnp.zeros((8, d), F32).at[0].set(c_ctx).at[1:1 + nb_l].set(c)

    def grp_fn(tm):
        nc = m_c // tm
        per = l_l // tm
        return lambda i: jnp.where(i < nc, 0, 1 + (i - nc) // per)

    w_in_p = _prep_w_in(w_in)
    w_out_b = w_out.astype(BF16)
    w_uq_p = _prep_heads(mla_w_uq, MLA_NOPE, MLA_ROPE)
    w_ukv_p = _prep_heads(mla_w_ukv, MLA_NOPE, HEAD_DIM)
    rw_p = jnp.pad(router_w, ((0, 0), (0, 0), (0, LANES - n_exp)))
    rb_p = jnp.pad(router_b, ((0, 0), (0, LANES - n_exp)), constant_values=-jnp.inf)
    tables_c = _dft_tables(l_c)
    tables_l = _dft_tables(l_l)
    lg_all = jax.nn.log_sigmoid(ret_decay.astype(F32))
    ckv_cache_rows = cache_mla_ckv.reshape(nb_l * depth * past, MLA_KV_LORA)

    new_k, new_v, new_ckv, new_kr, new_s = [], [], [], [], []
    for layer in range(depth):
        lw = {"hy_conv_w": hy_conv_w[layer], "hy_conv_b": hy_conv_b[layer],
              "hy_filt_w1": hy_filt_w1[layer], "hy_filt_b1": hy_filt_b1[layer],
              "hy_filt_w2": hy_filt_w2[layer], "hy_filt_b2": hy_filt_b2[layer],
              "hy_filt_w3": hy_filt_w3[layer], "hy_bias": hy_bias[layer]}
        mod = _modulation(cv, w_mod, b_mod, layer)
        sa, ca, ga, sf, cf, gf = [a.reshape(8, 1, d) for a in jnp.split(mod, 6, axis=-1)]

        proj = _norm_matmul(x, 0, d, w_in_p[layer], tm=512, tn=512, gain=norm_mix_pre[layer],
                            mod=(ca, sa), grp=grp_fn(512), name="proj_in")

        y_hy = jnp.concatenate([_hyena(proj, 0, nb_c, l_c, lw, tables_c),
                                _hyena(proj, m_c, nb_l, l_l, lw, tables_l)], axis=0)

        y_na_c, k_c, v_c = _na_context(proj, nb_c, l_c)
        y_na_l = _na_latent(proj, m_c, nb_l, l_l, cache_na_k, cache_na_v, layer, _na_bias_table(na_rpb[layer]))
        y_na = jnp.concatenate([y_na_c, y_na_l], axis=0)
        new_k.append(k_c)
        new_v.append(v_c)

        q = _norm_matmul(proj, COL_QC // MLA_Q_LORA, MLA_Q_LORA, w_uq_p[layer], tm=512, tn=512,
                         gain=mla_q_norm[layer], name="mla_q")
        kv, ckv_n = _norm_matmul(proj, COL_CKV // MLA_KV_LORA, MLA_KV_LORA, w_ukv_p[layer], tm=512, tn=1024,
                                 gain=mla_kv_norm[layer], emit_h=True, out_dtype=BF16, name="mla_kv")
        kv_cache = jnp.concatenate(
            [_norm_matmul(ckv_cache_rows, 0, MLA_KV_LORA, w_ukv_p[layer], tm=past, tn=1024, out_dtype=BF16,
                          xrow0=b * depth + layer, m=past, name="mla_kv_cache") for b in range(nb_l)], axis=0)
        new_ckv.append(ckv_n[:m_c].reshape(nb_c, l_c, MLA_KV_LORA))
        kr_c = proj[:m_c, COL_KROPE:COL_KROPE + MLA_ROPE]
        new_kr.append(kr_c.reshape(nb_c, l_c, MLA_ROPE))
        y_mla_c = _mla_attention(
            (q, pl.BlockSpec((l_c, GROUP_W), lambda b, i: (b, 0))),
            (q, pl.BlockSpec((l_c, N_HEADS * MLA_ROPE), lambda b, i: (b, GROUP_W // (N_HEADS * MLA_ROPE)))),
            [((kv, pl.BlockSpec((l_c, GROUP_W), lambda b, i: (b, 0))),
              (proj, pl.BlockSpec((l_c, LANES), lambda b, i: (b, COL_KROPE // LANES))),
              (kv, pl.BlockSpec((l_c, GROUP_W), lambda b, i: (b, 1))))],
            nb=nb_c, lq=l_c, tq=l_c, name="mla_context")
        q_l = q[m_c:]
        qr_l = _axial_rope(q_l[:, GROUP_W:].reshape(nb_l, l_l, N_HEADS, MLA_ROPE).transpose(0, 2, 1, 3))
        qr_l = qr_l.transpose(0, 2, 1, 3).reshape(m_l, N_HEADS * MLA_ROPE)
        kr_l = _axial_rope(proj[m_c:, COL_KROPE:COL_KROPE + MLA_ROPE].reshape(nb_l, l_l, MLA_ROPE))
        kr_l = kr_l.reshape(m_l, MLA_ROPE)
        tq = 256
        nq = l_l // tq
        rb_l = m_c // l_l
        y_mla_l = _mla_attention(
            (q, pl.BlockSpec((tq, GROUP_W), lambda b, i: (m_c // tq + b * nq + i, 0))),
            (qr_l, pl.BlockSpec((tq, N_HEADS * MLA_ROPE), lambda b, i: (b * nq + i, 0))),
            [((kv_cache, pl.BlockSpec((past, GROUP_W), lambda b, i: (b, 0))),
              (cache_mla_krope, pl.BlockSpec((None, None, past, MLA_ROPE), lambda b, i: (b, layer, 0, 0))),
              (kv_cache, pl.BlockSpec((past, GROUP_W), lambda b, i: (b, 1)))),
             ((kv, pl.BlockSpec((l_l, GROUP_W), lambda b, i: (rb_l + b, 0))),
              (kr_l, pl.BlockSpec((l_l, MLA_ROPE), lambda b, i: (b, 0))),
              (kv, pl.BlockSpec((l_l, GROUP_W), lambda b, i: (rb_l + b, 1))))],
            nb=nb_l, lq=l_l, tq=tq, name="mla_latent")
        y_mla = jnp.concatenate([y_mla_c, y_mla_l], axis=0)

        lg = lg_all[layer]
        y_ret_c, s_c = _retention(proj, 0, nb_c, l_c, l_c, lg, ret_gn[layer], emit_state=True)
        y_ret_l = _retention(proj, m_c, nb_l, l_l, 256, lg, ret_gn[layer], state=state_ret, layer=layer)
        y_ret = jnp.concatenate([y_ret_c, y_ret_l], axis=0)
        new_s.append(s_c)

        x1, h_ffn, top_i, top_g = _out_proj(
            (y_hy, y_na, y_mla, y_ret), w_out_b[layer], x, norm_mix_post[layer], ga, norm_ffn_pre[layer],
            cf, sf, rw_p[layer], rb_p[layer:layer + 1], grp_fn(256))

        y_rows, dest = _moe_experts(h_ffn, top_i[:, :TOP_K], top_g[:, :TOP_K], moe_w1, moe_b1, moe_w2, moe_b2, layer)
        ys = [jnp.take(y_rows, dest[:, kk], axis=0) for kk in range(TOP_K)]
        x = _combine(ys, x1, norm_ffn_post[layer], gf, grp_fn(128))

    return (x[:m_c].reshape(nb_c, l_c, d), x[m_c:].reshape(nb_l, l_l, d),
            jnp.stack(new_k, axis=1), jnp.stack(new_v, axis=1), jnp.stack(new_ckv, axis=1),
            jnp.stack(new_kr, axis=1), jnp.stack(new_s, axis=1))
```

```python
import functools
import math

import jax
import jax.numpy as jnp
from jax import lax
from jax.experimental import pallas as pl
from jax.experimental.pallas import tpu as pltpu

F32 = jnp.float32
BF16 = jnp.bfloat16
SDS = jax.ShapeDtypeStruct

EPS = 1e-6
GRID_W = 64
HEAD_DIM = 128
N_HEADS = 8
GROUP_W = N_HEADS * HEAD_DIM
HY_BANDS = 16
HY_TARGET = 1e-2
HY_FAST_DECAY = 0.3
HY_SLOW_DECAY = 1.5
NA_WR = 8
NA_WC = 16
MLA_Q_LORA = 768
MLA_KV_LORA = 512
MLA_ROPE = 64
MLA_NOPE = 128
MLA_SCALE = (MLA_NOPE + MLA_ROPE) ** -0.5
ROPE_BASE = 10000.0
TOP_K = 4
SWIGLU_LIMIT = 7.0
SWIGLU_ALPHA = 1.702
LANES = 128

COL_HY = 0
COL_NA = 3072
COL_RET = 6144
COL_CKV = 10240
COL_QC = 10752
COL_KROPE = 11520
N_PROJ = 11776

VMEM_LIMIT_MB = 56


def _cp(sem, vmem_mb=48):
    return pltpu.CompilerParams(dimension_semantics=sem, vmem_limit_bytes=min(vmem_mb, VMEM_LIMIT_MB) * 2**20)


def _dot(a, b):
    return jnp.dot(a, b, preferred_element_type=F32)


def _dot_nt(a, b):
    return lax.dot_general(a, b, (((1,), (1,)), ((), ())), preferred_element_type=F32)


def _dot_tn(a, b):
    return lax.dot_general(a, b, (((0,), (0,)), ((), ())), preferred_element_type=F32)


def _silu(x):
    return x * jax.nn.sigmoid(x)


def _mod_body(c_ref, w_ref, b_ref, o_ref):
    s = _silu(c_ref[...])
    o_ref[...] = _dot(s.astype(BF16), w_ref[...].astype(BF16)) + b_ref[...]


def _modulation(cv, w_mod, b_mod, layer):
    _, d, n = w_mod.shape
    tn = 512
    return pl.pallas_call(
        _mod_body,
        grid=(n // tn,),
        in_specs=[pl.BlockSpec((8, d), lambda j: (0, 0)),
                  pl.BlockSpec((None, d, tn), lambda j: (layer, 0, j)),
                  pl.BlockSpec((None, 1, tn), lambda j: (layer, 0, j))],
        out_specs=pl.BlockSpec((8, tn), lambda j: (0, j)),
        out_shape=SDS((8, n), F32),
        compiler_params=_cp(("arbitrary",), 40),
        name="modulation",
    )(cv, w_mod, b_mod.reshape(b_mod.shape[0], 1, n))


def _norm_matmul_body(*refs, do_norm, has_mod, emit_h):
    it = iter(refs)
    x_ref = next(it)
    g_ref = next(it) if do_norm else None
    sc_ref = next(it) if has_mod else None
    sh_ref = next(it) if has_mod else None
    w_ref = next(it)
    o_ref = next(it)
    h_ref = next(it) if emit_h else None
    hs_ref = next(it)

    @pl.when(pl.program_id(1) == 0)
    def _():
        h = x_ref[...].astype(F32)
        if do_norm:
            h = h * lax.rsqrt(jnp.mean(h * h, axis=-1, keepdims=True) + EPS) * g_ref[...]
        if has_mod:
            h = h * (1.0 + sc_ref[...]) + sh_ref[...]
        if emit_h:
            h_ref[...] = h
        hs_ref[...] = h.astype(BF16)

    o_ref[...] = _dot(hs_ref[...], w_ref[...].astype(BF16)).astype(o_ref.dtype)


def _norm_matmul(x, xcol, k, w, *, tm, tn, gain=None, mod=None, grp=None, emit_h=False,
                 out_dtype=F32, xrow0=0, m=None, name="norm_matmul"):
    m = x.shape[0] if m is None else m
    n = w.shape[1]
    do_norm = gain is not None
    has_mod = mod is not None
    in_specs = [pl.BlockSpec((tm, k), lambda i, j: (i + xrow0, xcol))]
    args = [x]
    if do_norm:
        in_specs.append(pl.BlockSpec((1, k), lambda i, j: (0, 0)))
        args.append(gain.reshape(1, k))
    if has_mod:
        for a in mod:
            in_specs.append(pl.BlockSpec((None, 1, k), lambda i, j: (grp(i), 0, 0)))
            args.append(a)
    in_specs.append(pl.BlockSpec((k, tn), lambda i, j: (0, j)))
    args.append(w)
    out_specs = [pl.BlockSpec((tm, tn), lambda i, j: (i, j))]
    out_shape = [SDS((m, n), out_dtype)]
    if emit_h:
        out_specs.append(pl.BlockSpec((tm, k), lambda i, j: (i, 0)))
        out_shape.append(SDS((m, k), F32))
    res = pl.pallas_call(
        functools.partial(_norm_matmul_body, do_norm=do_norm, has_mod=has_mod, emit_h=emit_h),
        grid=(m // tm, n // tn),
        in_specs=in_specs, out_specs=out_specs, out_shape=out_shape,
        scratch_shapes=[pltpu.VMEM((tm, k), BF16)],
        compiler_params=_cp(("arbitrary", "arbitrary"), 48),
        name=name,
    )(*args)
    return res if emit_h else res[0]


def _dft_tables(l):
    n = 2 * l
    f = jnp.arange(l, dtype=jnp.int32)
    ang = ((f[:, None] * f[None, :]) % n).astype(F32) * (2.0 * math.pi / n)
    fc = jnp.cos(ang)
    alt = jnp.where(f % 2 == 0, 1.0, -1.0).astype(F32)
    fs = jnp.where(f[:, None] == 0, alt[None, :], jnp.sin(ang))
    return fc.astype(BF16), fs.astype(BF16), fs.T.astype(BF16)


def _hy_spec_body(fc_ref, fs_ref, hf_ref, hb_ref, a_ref, b_ref, *, l):
    hf = hf_ref[...]
    row = lax.broadcasted_iota(jnp.int32, hf.shape, 0)
    hb = jnp.where(row == 0, 0.0, hb_ref[...])
    a = _dot(fc_ref[...], (hf + hb).astype(BF16))
    p1 = _dot(fs_ref[...], hf.astype(BF16))
    p2 = _dot(fs_ref[...], hb.astype(BF16))
    orow = lax.broadcasted_iota(jnp.int32, a.shape, 0)
    first = jnp.logical_and(pl.program_id(0) == 0, orow == 0)
    wgt = jnp.where(first, 0.5 / l, 1.0 / l)
    a_ref[...] = wgt * a
    b_ref[...] = wgt * jnp.where(first, p1 + p2, p1 - p2)


def _hy_spectra(fc, fs, hf, hb):
    l, c2 = hf.shape
    tf = min(l, 256)
    tc = 512
    return pl.pallas_call(
        functools.partial(_hy_spec_body, l=l),
        grid=(l // tf, c2 // tc),
        in_specs=[pl.BlockSpec((tf, l), lambda i, j: (i, 0)),
                  pl.BlockSpec((tf, l), lambda i, j: (i, 0)),
                  pl.BlockSpec((l, tc), lambda i, j: (0, j)),
                  pl.BlockSpec((l, tc), lambda i, j: (0, j))],
        out_specs=[pl.BlockSpec((tf, tc), lambda i, j: (i, j))] * 2,
        out_shape=[SDS((l, c2), F32)] * 2,
        compiler_params=_cp(("arbitrary", "arbitrary"), 40),
        name="hyena_filter_spectra",
    )(fc, fs, hf, hb)


def _short_conv(x, w, b):
    l = x.shape[0]
    row = lax.broadcasted_iota(jnp.int32, x.shape, 0)
    xm = jnp.where(row == 0, 0.0, pltpu.roll(x, 1, 0))
    xp = jnp.where(row == l - 1, 0.0, pltpu.roll(x, l - 1, 0))
    return xm * w[0:1, :] + x * w[1:2, :] + xp * w[2:3, :] + b


def _hy_conv_body(*refs, first_order, nf):
    it = iter(refs)
    z_ref = next(it)
    zw_ref = next(it) if first_order else None
    zb_ref = next(it) if first_order else None
    gt_ref, gw_ref, gb_ref, skip_ref = next(it), next(it), next(it), next(it)
    ah_ref, bh_ref, fc_ref, fs_ref, gc_ref, gs_ref = (next(it) for _ in range(6))
    o_ref = next(it)
    zf_scr, zb_scr, acc = next(it), next(it), next(it)
    fb = pl.program_id(2)

    @pl.when(fb == 0)
    def _():
        z = z_ref[...].astype(F32)
        if first_order:
            z = _short_conv(z, zw_ref[...], zb_ref[...])
        zf_scr[...] = z
        zb_scr[...] = z.astype(BF16)
        acc[...] = jnp.zeros_like(acc)

    zb = zb_scr[...]
    uc = _dot(fc_ref[...], zb)
    us = _dot(fs_ref[...], zb)
    ah = ah_ref[...]
    bh = bh_ref[...]
    row = lax.broadcasted_iota(jnp.int32, uc.shape, 0)
    first = jnp.logical_and(fb == 0, row == 0)
    usb = us * bh
    yc = uc * ah - jnp.where(first, 0.0, usb)
    ys = jnp.where(first, usb, uc * bh + us * ah)
    acc[...] += _dot(gc_ref[...], yc.astype(BF16)) + _dot(gs_ref[...], ys.astype(BF16))

    @pl.when(fb == nf - 1)
    def _():
        gate = _short_conv(gt_ref[...].astype(F32), gw_ref[...], gb_ref[...])
        o_ref[...] = (gate * (acc[...] + zf_scr[...] * skip_ref[...])).astype(o_ref.dtype)


def _hy_conv(z, gate_col, proj, rb, conv_w, conv_b, skip, ah, bh, spec_col0, tables, *,
             nb, l, z_col, out_dtype):
    fc, fs, gs = tables
    first_order = z_col is not None
    tc = 512 if l > 512 else GROUP_W
    tf = min(l, 256)
    nf = l // tf

    def proj_specs(col):
        return [pl.BlockSpec((l, tc), lambda b, c, f: (b + rb, col // tc + c)),
                pl.BlockSpec((3, tc), lambda b, c, f: (0, (col - COL_HY) // tc + c)),
                pl.BlockSpec((1, tc), lambda b, c, f: (0, (col - COL_HY) // tc + c))]

    if first_order:
        in_specs = proj_specs(z_col)
        args = [proj, conv_w, conv_b]
    else:
        in_specs = [pl.BlockSpec((l, tc), lambda b, c, f: (b, c))]
        args = [z]
    in_specs += proj_specs(gate_col)
    args += [proj, conv_w, conv_b]
    in_specs += [pl.BlockSpec((1, tc), lambda b, c, f: (0, c)),
                 pl.BlockSpec((tf, tc), lambda b, c, f: (f, spec_col0 // tc + c)),
                 pl.BlockSpec((tf, tc), lambda b, c, f: (f, spec_col0 // tc + c)),
                 pl.BlockSpec((tf, l), lambda b, c, f: (f, 0)),
                 pl.BlockSpec((tf, l), lambda b, c, f: (f, 0)),
                 pl.BlockSpec((l, tf), lambda b, c, f: (0, f)),
                 pl.BlockSpec((l, tf), lambda b, c, f: (0, f))]
    args += [skip, ah, bh, fc, fs, fc, gs]
    return pl.pallas_call(
        functools.partial(_hy_conv_body, first_order=first_order, nf=nf),
        grid=(nb, GROUP_W // tc, nf),
        in_specs=in_specs,
        out_specs=pl.BlockSpec((l, tc), lambda b, c, f: (b, c)),
        out_shape=SDS((nb * l, GROUP_W), out_dtype),
        scratch_shapes=[pltpu.VMEM((l, tc), F32), pltpu.VMEM((l, tc), BF16), pltpu.VMEM((l, tc), F32)],
        compiler_params=_cp(("arbitrary", "arbitrary", "arbitrary"), 52),
        name="hyena_conv",
    )(*args)


def _hyena_filters(l, w1, b1, w2, b2, w3):
    hp = lax.Precision.HIGHEST
    pos = jnp.arange(l, dtype=F32)
    t = pos / l
    bands = jnp.linspace(1e-4, HY_BANDS - 1, HY_BANDS, dtype=F32)
    ang = (2.0 * math.pi * t)[:, None] * bands[None, :]
    z = jnp.concatenate([t[:, None], jnp.cos(ang), -jnp.sin(ang)], axis=-1)
    h = jnp.sin(jnp.dot(z, w1, precision=hp) + b1)
    h = jnp.sin(jnp.dot(h, w2, precision=hp) + b2)
    h = jnp.dot(h, w3, precision=hp)
    deltas = jnp.abs(jnp.linspace(math.log(HY_TARGET) / HY_FAST_DECAY,
                                  math.log(HY_TARGET) / HY_SLOW_DECAY, GROUP_W, dtype=F32))
    window = jnp.exp(-t[:, None] * deltas[None, :])
    filt = h.reshape(l, 2, 2, GROUP_W) * window[:, None, None, :]
    return filt[:, :, 0, :].reshape(l, 2 * GROUP_W), filt[:, :, 1, :].reshape(l, 2 * GROUP_W)


def _hyena(proj, row0, nb, l, lw, tables):
    fc, fs, gs = tables
    hf, hb = _hyena_filters(l, lw["hy_filt_w1"], lw["hy_filt_b1"], lw["hy_filt_w2"], lw["hy_filt_b2"],
                            lw["hy_filt_w3"])
    ah, bh = _hy_spectra(fc, fs, hf, hb)
    cw = lw["hy_conv_w"]
    cb = lw["hy_conv_b"].reshape(1, -1)
    skip = lw["hy_bias"]
    rb = row0 // l
    z1 = _hy_conv(None, COL_HY, proj, rb, cw, cb, skip[0:1], ah, bh, 0, tables,
                  nb=nb, l=l, z_col=COL_HY + 2 * GROUP_W, out_dtype=F32)
    return _hy_conv(z1, COL_HY + GROUP_W, proj, rb, cw, cb, skip[1:2], ah, bh, GROUP_W, tables,
                    nb=nb, l=l, z_col=None, out_dtype=BF16)


def _softmax_attend(scores, values):
    m = functools.reduce(jnp.maximum, [jnp.max(s, axis=-1, keepdims=True) for s in scores])
    ps = [jnp.exp(s - m) for s in scores]
    den = functools.reduce(jnp.add, [jnp.sum(p, axis=-1, keepdims=True) for p in ps])
    o = functools.reduce(jnp.add, [_dot(p.astype(BF16), v) for p, v in zip(ps, values)])
    return o / den


def _na_ctx_body(q_ref, k_ref, v_ref, o_ref, kc_ref, vc_ref):
    scale = HEAD_DIM ** -0.5
    for h in range(N_HEADS):
        hs = slice(h * HEAD_DIM, (h + 1) * HEAD_DIM)
        k = k_ref[:, hs]
        v = v_ref[:, hs]
        kc_ref[h] = k
        vc_ref[h] = v
        s = _dot_nt(q_ref[:, hs].astype(BF16), k.astype(BF16)) * scale
        o_ref[:, hs] = _softmax_attend([s], [v.astype(BF16)]).astype(o_ref.dtype)


def _na_context(proj, nb, l):
    c0 = COL_NA // GROUP_W
    return pl.pallas_call(
        _na_ctx_body,
        grid=(nb,),
        in_specs=[pl.BlockSpec((l, GROUP_W), lambda b: (b, c0)),
                  pl.BlockSpec((l, GROUP_W), lambda b: (b, c0 + 1)),
                  pl.BlockSpec((l, GROUP_W), lambda b: (b, c0 + 2))],
        out_specs=[pl.BlockSpec((l, GROUP_W), lambda b: (b, 0)),
                   pl.BlockSpec((None, N_HEADS, l, HEAD_DIM), lambda b: (b, 0, 0, 0)),
                   pl.BlockSpec((None, N_HEADS, l, HEAD_DIM), lambda b: (b, 0, 0, 0))],
        out_shape=[SDS((nb * l, GROUP_W), BF16),
                   SDS((nb, N_HEADS, l, HEAD_DIM), F32),
                   SDS((nb, N_HEADS, l, HEAD_DIM), F32)],
        compiler_params=_cp(("arbitrary",), 32),
        name="na_context",
    )(proj, proj, proj)


def _na_bias_table(rpb):
    cidx = jnp.arange(GRID_W)
    col_start = jnp.clip(cidx - NA_WC // 2, 0, GRID_W - NA_WC)
    col_in = (cidx[None, :] >= col_start[:, None]) & (cidx[None, :] < col_start[:, None] + NA_WC)
    col_off = jnp.clip(cidx[None, :] - cidx[:, None], -(NA_WC - 1), NA_WC - 1) + (NA_WC - 1)
    t = rpb[:, :, col_off].astype(F32)
    t = jnp.where(col_in[None, None], t, -jnp.inf)
    return t.transpose(0, 2, 1, 3).reshape(rpb.shape[0], GRID_W, -1)


def _na_lat_body(q_ref, k_ref, v_ref, kc_ref, vc_ref, bt_ref, o_ref, *, rows):
    scale = HEAD_DIM ** -0.5
    wr = min(NA_WR, rows)
    kcx = kc_ref[...].astype(BF16)
    vcx = vc_ref[...].astype(BF16)
    nloc = wr * GRID_W
    bt = bt_ref[...]
    for r in range(rows):
        rs = min(max(r - wr // 2, 0), rows - wr)
        off = rs - r + (NA_WR - 1)
        q = q_ref[r * GRID_W:(r + 1) * GRID_W, :].astype(BF16)
        kl = k_ref[rs * GRID_W:rs * GRID_W + nloc, :].astype(BF16)
        vl = v_ref[rs * GRID_W:rs * GRID_W + nloc, :].astype(BF16)
        s_loc = _dot_nt(q, kl) * scale + bt[:, off * GRID_W:off * GRID_W + nloc]
        s_ctx = _dot_nt(q, kcx) * scale
        o_ref[r * GRID_W:(r + 1) * GRID_W, :] = _softmax_attend([s_loc, s_ctx], [vl, vcx]).astype(o_ref.dtype)


def _na_latent(proj, row0, nb, l, cache_k, cache_v, layer, bias_table):
    rows = l // GRID_W
    rb = row0 // l
    c0 = COL_NA // HEAD_DIM
    past = cache_k.shape[3]
    cspec = pl.BlockSpec((None, None, None, past, HEAD_DIM), lambda b, h: (b, layer, h, 0, 0))
    return pl.pallas_call(
        functools.partial(_na_lat_body, rows=rows),
        grid=(nb, N_HEADS),
        in_specs=[pl.BlockSpec((l, HEAD_DIM), lambda b, h: (b + rb, c0 + h)),
                  pl.BlockSpec((l, HEAD_DIM), lambda b, h: (b + rb, c0 + N_HEADS + h)),
                  pl.BlockSpec((l, HEAD_DIM), lambda b, h: (b + rb, c0 + 2 * N_HEADS + h)),
                  cspec, cspec,
                  pl.BlockSpec((None, GRID_W, bias_table.shape[2]), lambda b, h: (h, 0, 0))],
        out_specs=pl.BlockSpec((l, HEAD_DIM), lambda b, h: (b, h)),
        out_shape=SDS((nb * l, GROUP_W), BF16),
        compiler_params=_cp(("arbitrary", "arbitrary"), 32),
        name="na_latent",
    )(proj, proj, proj, cache_k, cache_v, bias_table)


def _mla_attn_body(*refs, nseg):
    qn_ref, qr_ref = refs[0], refs[1]
    segs = [refs[2 + 3 * i: 5 + 3 * i] for i in range(nseg)]
    o_ref = refs[2 + 3 * nseg]
    krs = [kr_ref[...][:, :MLA_ROPE].astype(BF16) for _, kr_ref, _ in segs]
    qr_all = qr_ref[...].astype(BF16)
    for h in range(N_HEADS):
        hs = slice(h * HEAD_DIM, (h + 1) * HEAD_DIM)
        qn = qn_ref[:, hs].astype(BF16)
        qr = qr_all[:, h * MLA_ROPE:(h + 1) * MLA_ROPE]
        scores = [(_dot_nt(qn, kn_ref[:, hs].astype(BF16)) + _dot_nt(qr, kr)) * MLA_SCALE
                  for (kn_ref, _, _), kr in zip(segs, krs)]
        vals = [v_ref[:, hs].astype(BF16) for _, _, v_ref in segs]
        o_ref[:, hs] = _softmax_attend(scores, vals).astype(o_ref.dtype)


def _mla_attention(qn, qr, segs, *, nb, lq, tq, name):
    arrays = [qn[0], qr[0]]
    specs = [qn[1], qr[1]]
    for s in segs:
        for a, sp in s:
            arrays.append(a)
            specs.append(sp)
    return pl.pallas_call(
        functools.partial(_mla_attn_body, nseg=len(segs)),
        grid=(nb, lq // tq),
        in_specs=specs,
        out_specs=pl.BlockSpec((tq, GROUP_W), lambda b, i: (b * (lq // tq) + i, 0)),
        out_shape=SDS((nb * lq, GROUP_W), BF16),
        compiler_params=_cp(("arbitrary", "arbitrary"), 48),
        name=name,
    )(*arrays)


def _axial_rope(x):
    l = x.shape[-2]
    pos = jnp.arange(l)
    half = MLA_ROPE // 2
    freqs = jnp.power(ROPE_BASE, -jnp.arange(0, half, 2, dtype=F32) / half)

    def rot(xa, p):
        ang = p.astype(F32)[:, None] * freqs[None, :]
        cos, sin = jnp.cos(ang), jnp.sin(ang)
        a1, a2 = jnp.split(xa, 2, axis=-1)
        return jnp.concatenate([a1 * cos - a2 * sin, a2 * cos + a1 * sin], axis=-1)

    return jnp.concatenate([rot(x[..., :half], pos // GRID_W), rot(x[..., half:], pos % GRID_W)], axis=-1)


def _ret_body(*refs, l, tq, has_state, emit_state):
    it = iter(refs)
    lg_ref = next(it)
    q_ref, k_ref, v_ref, g_ref, gn_ref = (next(it) for _ in range(5))
    s0_ref = next(it) if has_state else None
    o_ref = next(it)
    so_ref = next(it) if emit_state else None
    t0 = pl.program_id(1) * tq
    ti = lax.broadcasted_iota(jnp.int32, (tq, l), 0) + t0
    si = lax.broadcasted_iota(jnp.int32, (tq, l), 1)
    d = (ti - si).astype(F32)
    tcol = (lax.broadcasted_iota(jnp.int32, (tq, 1), 0) + t0).astype(F32)
    scol = lax.broadcasted_iota(jnp.int32, (l, 1), 0).astype(F32)
    kscale = HEAD_DIM ** -0.5
    for h in range(N_HEADS):
        hs = slice(h * HEAD_DIM, (h + 1) * HEAD_DIM)
        lgf = lg_ref[0, h]
        lgb = lg_ref[1, h]
        q = q_ref[:, hs]
        k = k_ref[:, hs] * kscale
        v = v_ref[:, hs].astype(BF16)
        s = _dot_nt(q.astype(BF16), k.astype(BF16))
        w = jnp.exp(jnp.where(d > 0, lgf, -lgb) * d)
        w = jnp.where(d == 0, 2.0, w)
        o = _dot((s * w).astype(BF16), v)
        if has_state:
            qf = q * jnp.exp(lgf * (tcol + 1.0))
            qb = q * jnp.exp(lgb * (l - tcol))
            o = o + _dot(qf.astype(BF16), s0_ref[0, h].astype(BF16)) + _dot(qb.astype(BF16), s0_ref[1, h].astype(BF16))
        o = o * lax.rsqrt(jnp.mean(o * o, axis=-1, keepdims=True) + EPS) * gn_ref[:, hs]
        o_ref[:, hs] = (o * _silu(g_ref[:, hs])).astype(o_ref.dtype)
        if emit_state:
            kf = k * jnp.exp(lgf * (l - 1.0 - scol))
            kb = k * jnp.exp(lgb * scol)
            so_ref[0, h] = _dot_tn(kf.astype(BF16), v)
            so_ref[1, h] = _dot_tn(kb.astype(BF16), v)


def _retention(proj, row0, nb, l, tq, lg, gn, state=None, layer=0, emit_state=False):
    assert not emit_state or tq == l
    nq = l // tq
    rbq = row0 // tq
    rbl = row0 // l
    c0 = COL_RET // GROUP_W
    in_specs = [pl.BlockSpec(memory_space=pltpu.SMEM),
                pl.BlockSpec((tq, GROUP_W), lambda b, i: (rbq + b * nq + i, c0)),
                pl.BlockSpec((l, GROUP_W), lambda b, i: (rbl + b, c0 + 1)),
                pl.BlockSpec((l, GROUP_W), lambda b, i: (rbl + b, c0 + 2)),
                pl.BlockSpec((tq, GROUP_W), lambda b, i: (rbq + b * nq + i, c0 + 3)),
                pl.BlockSpec((1, GROUP_W), lambda b, i: (0, 0))]
    args = [lg, proj, proj, proj, proj, gn.reshape(1, GROUP_W)]
    if state is not None:
        in_specs.append(pl.BlockSpec((None, None, 2, N_HEADS, HEAD_DIM, HEAD_DIM),
                                     lambda b, i: (b, layer, 0, 0, 0, 0)))
        args.append(state)
    out_specs = [pl.BlockSpec((tq, GROUP_W), lambda b, i: (b * nq + i, 0))]
    out_shape = [SDS((nb * l, GROUP_W), BF16)]
    if emit_state:
        out_specs.append(pl.BlockSpec((None, 2, N_HEADS, HEAD_DIM, HEAD_DIM), lambda b, i: (b, 0, 0, 0, 0)))
        out_shape.append(SDS((nb, 2, N_HEADS, HEAD_DIM, HEAD_DIM), F32))
    res = pl.pallas_call(
        functools.partial(_ret_body, l=l, tq=tq, has_state=state is not None, emit_state=emit_state),
        grid=(nb, nq),
        in_specs=in_specs, out_specs=out_specs, out_shape=out_shape,
        compiler_params=_cp(("arbitrary", "arbitrary"), 48),
        name="retention",
    )(*args)
    return res if emit_state else res[0]


def _out_matmul_body(y0_ref, y1_ref, y2_ref, y3_ref, w_ref, o_ref):
    acc = _dot(y0_ref[...], w_ref[0 * GROUP_W:1 * GROUP_W, :])
    acc += _dot(y1_ref[...], w_ref[1 * GROUP_W:2 * GROUP_W, :])
    acc += _dot(y2_ref[...], w_ref[2 * GROUP_W:3 * GROUP_W, :])
    acc += _dot(y3_ref[...], w_ref[3 * GROUP_W:4 * GROUP_W, :])
    o_ref[...] = acc


def _out_matmul(ys, w_out, *, tm=512, tn=1024):
    m = ys[0].shape[0]
    d = w_out.shape[1]
    yspec = pl.BlockSpec((tm, GROUP_W), lambda j, i: (i, 0))
    return pl.pallas_call(
        _out_matmul_body,
        grid=(d // tn, m // tm),
        in_specs=[yspec, yspec, yspec, yspec, pl.BlockSpec((4 * GROUP_W, tn), lambda j, i: (0, j))],
        out_specs=pl.BlockSpec((tm, tn), lambda j, i: (i, j)),
        out_shape=SDS((m, d), F32),
        compiler_params=_cp(("arbitrary", "arbitrary"), 40),
        name="out_proj",
    )(*ys, w_out)


def _topk(logits):
    lane = lax.broadcasted_iota(jnp.int32, logits.shape, 1)
    work = logits
    idx_out = jnp.zeros(logits.shape, jnp.int32)
    val_out = jnp.full(logits.shape, -jnp.inf, F32)
    chosen = jnp.zeros(logits.shape, F32)
    idxs = []
    for k in range(TOP_K):
        m = jnp.max(work, axis=-1, keepdims=True)
        idx = jnp.min(jnp.where(work == m, lane, LANES), axis=-1, keepdims=True)
        idx_out = jnp.where(lane == k, idx, idx_out)
        val_out = jnp.where(lane == k, m, val_out)
        chosen = jnp.where(lane == idx, 1.0, chosen)
        work = jnp.where(lane == idx, -jnp.inf, work)
        idxs.append(idx)
    e = jnp.exp(val_out - jnp.max(val_out, axis=-1, keepdims=True))
    return idx_out, e / jnp.sum(e, axis=-1, keepdims=True), chosen, idxs


def _post_router_body(yp_ref, x_ref, gpost_ref, ga_ref, gpre_ref, cf_ref, sf_ref, rwh_ref, rwl_ref, rb_ref,
                      x1_ref, h_ref, ti_ref, tg_ref, pos_ref, cnt_ref, carry):
    @pl.when(pl.program_id(0) == 0)
    def _():
        carry[...] = jnp.zeros_like(carry)

    y = yp_ref[...]
    y = y * lax.rsqrt(jnp.mean(y * y, axis=-1, keepdims=True) + EPS) * gpost_ref[...]
    x1 = x_ref[...] + ga_ref[...] * y
    x1_ref[...] = x1
    h = x1 * lax.rsqrt(jnp.mean(x1 * x1, axis=-1, keepdims=True) + EPS) * gpre_ref[...]
    h = h * (1.0 + cf_ref[...]) + sf_ref[...]
    h_ref[...] = h.astype(h_ref.dtype)
    h_hi = h.astype(BF16)
    h_lo = (h - h_hi.astype(F32)).astype(BF16)
    logits = _dot(h_hi, rwh_ref[...]) + _dot(h_lo, rwh_ref[...]) + _dot(h_hi, rwl_ref[...]) + rb_ref[...]
    idx_out, gates, chosen, idxs = _topk(logits)
    ti_ref[...] = idx_out
    tg_ref[...] = gates
    tm = chosen.shape[0]
    r = lax.broadcasted_iota(jnp.int32, (tm, tm), 0)
    c = lax.broadcasted_iota(jnp.int32, (tm, tm), 1)
    before = _dot(jnp.where(c < r, 1.0, 0.0).astype(BF16), chosen.astype(BF16)) + carry[0:1, :]
    lane = lax.broadcasted_iota(jnp.int32, chosen.shape, 1)
    pos = jnp.zeros(chosen.shape, jnp.int32)
    for k, idx in enumerate(idxs):
        pk = jnp.sum(jnp.where(lane == idx, before, 0.0), axis=-1, keepdims=True)
        pos = jnp.where(lane == k, pk.astype(jnp.int32), pos)
    pos_ref[...] = pos
    carry[...] = carry[...] + jnp.sum(chosen, axis=0, keepdims=True)
    cnt_ref[...] = carry[...]


def _post_router(yp, x, gpost, ga, gpre, cf, sf, rw_hi, rw_lo, rb, grp, *, tm=256):
    m, d = x.shape
    row = pl.BlockSpec((1, d), lambda i: (0, 0))
    modspec = pl.BlockSpec((None, 1, d), lambda i: (grp(i), 0, 0))
    full = pl.BlockSpec((tm, d), lambda i: (i, 0))
    small = pl.BlockSpec((tm, LANES), lambda i: (i, 0))
    rwspec = pl.BlockSpec((d, LANES), lambda i: (0, 0))
    return pl.pallas_call(
        _post_router_body,
        grid=(m // tm,),
        in_specs=[full, full, row, modspec, row, modspec, modspec, rwspec, rwspec,
                  pl.BlockSpec((1, LANES), lambda i: (0, 0))],
        out_specs=[full, full, small, small, small, pl.BlockSpec((8, LANES), lambda i: (0, 0))],
        out_shape=[SDS((m, d), F32), SDS((m, d), BF16), SDS((m, LANES), jnp.int32), SDS((m, LANES), F32),
                   SDS((m, LANES), jnp.int32), SDS((8, LANES), F32)],
        scratch_shapes=[pltpu.VMEM((8, LANES), F32)],
        compiler_params=_cp(("arbitrary",), 52),
        name="post_norm_router",
    )(yp, x, gpost.reshape(1, d), ga, gpre.reshape(1, d), cf, sf, rw_hi, rw_lo, rb)


def _deinterleave(hh):
    tm, tn = hh.shape
    lane = lax.broadcasted_iota(jnp.int32, (tm, LANES), 1)
    half = LANES // 2
    idx = jnp.where(lane < half, 2 * lane, 2 * (lane - half) + 1)
    ev, od = [], []
    for c in range(tn // (2 * LANES)):
        g0 = jnp.take_along_axis(hh[:, (2 * c) * LANES:(2 * c + 1) * LANES], idx, axis=1)
        g1 = jnp.take_along_axis(hh[:, (2 * c + 1) * LANES:(2 * c + 2) * LANES], idx, axis=1)
        ev.append(jnp.where(lane < half, g0, pltpu.roll(g1, half, 1)))
        od.append(jnp.where(lane < half, pltpu.roll(g0, half, 1), g1))
    return jnp.concatenate(ev, axis=1), jnp.concatenate(od, axis=1)


def _cast_weights(w_ref, wb_ref, chunk=256):
    k = w_ref.shape[0]

    def step(c, carry):
        r = pl.multiple_of(c * chunk, chunk)
        wb_ref[pl.ds(r, chunk), :] = w_ref[pl.ds(r, chunk), :].astype(BF16)
        return carry

    lax.fori_loop(0, k // chunk, step, 0)


def _moe_w1_body(se_ref, sj_ref, sb_ref, sf_ref, sv_ref, x_ref, w_ref, b_ref, o_ref, wb_ref):
    s = pl.program_id(0)

    @pl.when(sv_ref[s] == 1)
    def _():
        @pl.when(sf_ref[s] == 1)
        def _():
            _cast_weights(w_ref, wb_ref)

        hh = _dot(x_ref[...], wb_ref[...]) + b_ref[...]
        glu, lin = _deinterleave(hh)
        glu = jnp.minimum(glu, SWIGLU_LIMIT)
        lin = jnp.clip(lin, -SWIGLU_LIMIT, SWIGLU_LIMIT)
        o_ref[...] = (glu * jax.nn.sigmoid(SWIGLU_ALPHA * glu) * (lin + 1.0)).astype(o_ref.dtype)


def _moe_w2_body(se_ref, sj_ref, sb_ref, sf_ref, sv_ref, a_ref, w_ref, b_ref, o_ref, wb_ref):
    s = pl.program_id(0)

    @pl.when(sv_ref[s] == 1)
    def _():
        @pl.when(sf_ref[s] == 1)
        def _():
            _cast_weights(w_ref, wb_ref)

        o_ref[...] = _dot(a_ref[...], wb_ref[...]) + b_ref[...]


def _moe_schedule(counts, tm, nt, n_blocks):
    nb_e = (counts + tm - 1) // tm
    first_blk = jnp.cumsum(nb_e) - nb_e
    steps_e = nb_e * nt
    cs = jnp.cumsum(steps_e)
    total = cs[-1]
    n_steps = n_blocks * nt
    s = jnp.arange(n_steps, dtype=jnp.int32)
    sc = jnp.minimum(s, total - 1)
    e = jnp.minimum(jnp.sum((cs[None, :] <= sc[:, None]).astype(jnp.int32), axis=1), counts.shape[0] - 1)
    local = sc - (cs[e] - steps_e[e])
    nbe = jnp.maximum(nb_e[e], 1)
    j = local // nbe
    r = local % nbe
    blk = first_blk[e] + r
    valid = (s < total).astype(jnp.int32)
    first = ((r == 0) & (s < total)).astype(jnp.int32)
    return e, j.astype(jnp.int32), blk.astype(jnp.int32), first, valid


def _moe_experts(h, top_idx, pos, counts, w1, b1, w2, b2, layer, *, tm=256, tn1=1024, tn2=1024):
    m, d = h.shape
    n_exp, _, ff2 = w1.shape[1:]
    ff = ff2 // 2
    n_assign = m * TOP_K
    flat_e = top_idx.reshape(-1)
    padded = (counts + tm - 1) // tm * tm
    pad_start = jnp.cumsum(padded) - padded
    dest = (pad_start[flat_e] + pos.reshape(-1)).astype(jnp.int32)
    n_rows = n_assign + n_exp * tm
    n_blocks = n_rows // tm
    tok = jnp.arange(n_assign, dtype=jnp.int32) // TOP_K
    row_tok = jnp.zeros((n_rows,), jnp.int32).at[dest].set(tok, mode="promise_in_bounds", unique_indices=True)
    xs = h.at[row_tok].get(mode="promise_in_bounds")

    nt1 = ff2 // tn1
    sched1 = _moe_schedule(counts, tm, nt1, n_blocks)
    act = pl.pallas_call(
        _moe_w1_body,
        grid_spec=pltpu.PrefetchScalarGridSpec(
            num_scalar_prefetch=5,
            grid=(n_blocks * nt1,),
            in_specs=[pl.BlockSpec((tm, d), lambda s, se, sj, sb, sf, sv: (sb[s], 0)),
                      pl.BlockSpec((None, None, d, tn1), lambda s, se, sj, sb, sf, sv: (layer, se[s], 0, sj[s])),
                      pl.BlockSpec((None, None, 1, tn1), lambda s, se, sj, sb, sf, sv: (layer, se[s], 0, sj[s]))],
            out_specs=pl.BlockSpec((tm, tn1 // 2), lambda s, se, sj, sb, sf, sv: (sb[s], sj[s])),
            scratch_shapes=[pltpu.VMEM((d, tn1), BF16)]),
        out_shape=SDS((n_rows, ff), BF16),
        compiler_params=_cp(("arbitrary",), 56),
        name="moe_w1",
    )(*sched1, xs, w1, b1.reshape(b1.shape[0], n_exp, 1, ff2))

    nt2 = d // tn2
    sched2 = _moe_schedule(counts, tm, nt2, n_blocks)
    y_rows = pl.pallas_call(
        _moe_w2_body,
        grid_spec=pltpu.PrefetchScalarGridSpec(
            num_scalar_prefetch=5,
            grid=(n_blocks * nt2,),
            in_specs=[pl.BlockSpec((tm, ff), lambda s, se, sj, sb, sf, sv: (sb[s], 0)),
                      pl.BlockSpec((None, None, ff, tn2), lambda s, se, sj, sb, sf, sv: (layer, se[s], 0, sj[s])),
                      pl.BlockSpec((None, None, 1, tn2), lambda s, se, sj, sb, sf, sv: (layer, se[s], 0, sj[s]))],
            out_specs=pl.BlockSpec((tm, tn2), lambda s, se, sj, sb, sf, sv: (sb[s], sj[s])),
            scratch_shapes=[pltpu.VMEM((ff, tn2), BF16)]),
        out_shape=SDS((n_rows, d), F32),
        compiler_params=_cp(("arbitrary",), 48),
        name="moe_w2",
    )(*sched2, act, w2, b2.reshape(b2.shape[0], n_exp, 1, d))
    return y_rows, dest.reshape(m, TOP_K)


def _combine_body(y0_ref, y1_ref, y2_ref, y3_ref, tg_ref, x_ref, g_ref, gf_ref, o_ref):
    tg = tg_ref[...]
    y = (y0_ref[...] * tg[:, 0:1] + y1_ref[...] * tg[:, 1:2]) + (y2_ref[...] * tg[:, 2:3] + y3_ref[...] * tg[:, 3:4])
    inv = lax.rsqrt(jnp.mean(y * y, axis=-1, keepdims=True) + EPS)
    o_ref[...] = x_ref[...] + gf_ref[...] * (y * inv * g_ref[...])


def _combine(ys, top_gate, x, gpost, gf, grp, *, tm=128):
    m, d = x.shape
    full = pl.BlockSpec((tm, d), lambda i: (i, 0))
    return pl.pallas_call(
        _combine_body,
        grid=(m // tm,),
        in_specs=[full, full, full, full, pl.BlockSpec((tm, LANES), lambda i: (i, 0)), full,
                  pl.BlockSpec((1, d), lambda i: (0, 0)),
                  pl.BlockSpec((None, 1, d), lambda i: (grp(i), 0, 0))],
        out_specs=full,
        out_shape=SDS((m, d), F32),
        compiler_params=_cp(("arbitrary",), 56),
        name="moe_combine",
    )(*ys, top_gate, x, gpost.reshape(1, d), gf)


def _prep_w_in(w_in):
    hy_na = 6 * GROUP_W
    mla0 = hy_na
    ckv0 = mla0 + MLA_Q_LORA
    kr0 = ckv0 + MLA_KV_LORA
    ret0 = kr0 + MLA_ROPE
    pad = jnp.zeros(w_in.shape[:2] + (N_PROJ - COL_KROPE - MLA_ROPE,), w_in.dtype)
    parts = [w_in[..., :hy_na], w_in[..., ret0:], w_in[..., ckv0:kr0], w_in[..., mla0:ckv0],
             w_in[..., kr0:ret0], pad]
    return jnp.concatenate(parts, axis=-1).astype(BF16)


def _prep_heads(w, d_a, d_b):
    lead = w.shape[:-1]
    w = w.reshape(lead + (N_HEADS, d_a + d_b))
    a = w[..., :d_a].reshape(lead + (N_HEADS * d_a,))
    b = w[..., d_a:].reshape(lead + (N_HEADS * d_b,))
    return jnp.concatenate([a, b], axis=-1).astype(BF16)


def kernel(x_prompt, x_sample, cache_na_k, cache_na_v, cache_mla_ckv, cache_mla_krope, state_ret, c, c_ctx, w_mod, b_mod, norm_mix_pre, norm_mix_post, norm_ffn_pre, norm_ffn_post, w_in, hy_conv_w, hy_conv_b, hy_filt_w1, hy_filt_b1, hy_filt_w2, hy_filt_b2, hy_filt_w3, hy_bias, na_rpb, mla_q_norm, mla_w_uq, mla_kv_norm, mla_w_ukv, ret_decay, ret_gn, w_out, router_w, router_b, moe_w1, moe_b1, moe_w2, moe_b2):
    nb_c, l_c, d = x_prompt.shape
    nb_l, l_l, _ = x_sample.shape
    depth = w_in.shape[0]
    past = cache_na_k.shape[3]
    m_c = nb_c * l_c
    m_l = nb_l * l_l
    n_exp = router_w.shape[2]

    x = jnp.concatenate([x_prompt.reshape(m_c, d), x_sample.reshape(m_l, d)], axis=0)
    cv = jnp.zeros((8, d), F32).at[0].set(c_ctx).at[1:1 + nb_l].set(c)

    def grp_fn(tm):
        nc = m_c // tm
        per = l_l // tm
        return lambda i: jnp.where(i < nc, 0, 1 + (i - nc) // per)

    w_in_p = _prep_w_in(w_in)
    w_out_b = w_out.astype(BF16)
    w_uq_p = _prep_heads(mla_w_uq, MLA_NOPE, MLA_ROPE)
    w_ukv_p = _prep_heads(mla_w_ukv, MLA_NOPE, HEAD_DIM)
    rw_p = jnp.pad(router_w, ((0, 0), (0, 0), (0, LANES - n_exp)))
    rw_hi = rw_p.astype(BF16)
    rw_lo = (rw_p - rw_hi.astype(F32)).astype(BF16)
    rb_p = jnp.pad(router_b, ((0, 0), (0, LANES - n_exp)), constant_values=-jnp.inf)
    tables_c = _dft_tables(l_c)
    tables_l = _dft_tables(l_l)
    lg_all = jax.nn.log_sigmoid(ret_decay.astype(F32))
    ckv_cache_rows = cache_mla_ckv.reshape(nb_l * depth * past, MLA_KV_LORA)

    new_k, new_v, new_ckv, new_kr, new_s = [], [], [], [], []
    for layer in range(depth):
        lw = {"hy_conv_w": hy_conv_w[layer], "hy_conv_b": hy_conv_b[layer],
              "hy_filt_w1": hy_filt_w1[layer], "hy_filt_b1": hy_filt_b1[layer],
              "hy_filt_w2": hy_filt_w2[layer], "hy_filt_b2": hy_filt_b2[layer],
              "hy_filt_w3": hy_filt_w3[layer], "hy_bias": hy_bias[layer]}
        mod = _modulation(cv, w_mod, b_mod, layer)
        sa, ca, ga, sf, cf, gf = [a.reshape(8, 1, d) for a in jnp.split(mod, 6, axis=-1)]

        proj = _norm_matmul(x, 0, d, w_in_p[layer], tm=512, tn=512, gain=norm_mix_pre[layer],
                            mod=(ca, sa), grp=grp_fn(512), name="proj_in")

        y_hy = jnp.concatenate([_hyena(proj, 0, nb_c, l_c, lw, tables_c),
                                _hyena(proj, m_c, nb_l, l_l, lw, tables_l)], axis=0)

        y_na_c, k_c, v_c = _na_context(proj, nb_c, l_c)
        y_na_l = _na_latent(proj, m_c, nb_l, l_l, cache_na_k, cache_na_v, layer, _na_bias_table(na_rpb[layer]))
        y_na = jnp.concatenate([y_na_c, y_na_l], axis=0)
        new_k.append(k_c)
        new_v.append(v_c)

        q = _norm_matmul(proj, COL_QC // MLA_Q_LORA, MLA_Q_LORA, w_uq_p[layer], tm=512, tn=512,
                         gain=mla_q_norm[layer], name="mla_q")
        kv, ckv_n = _norm_matmul(proj, COL_CKV // MLA_KV_LORA, MLA_KV_LORA, w_ukv_p[layer], tm=512, tn=1024,
                                 gain=mla_kv_norm[layer], emit_h=True, out_dtype=BF16, name="mla_kv")
        kv_cache = jnp.concatenate(
            [_norm_matmul(ckv_cache_rows, 0, MLA_KV_LORA, w_ukv_p[layer], tm=past, tn=1024, out_dtype=BF16,
                          xrow0=b * depth + layer, m=past, name="mla_kv_cache") for b in range(nb_l)], axis=0)
        new_ckv.append(ckv_n[:m_c].reshape(nb_c, l_c, MLA_KV_LORA))
        kr_c = proj[:m_c, COL_KROPE:COL_KROPE + MLA_ROPE]
        new_kr.append(kr_c.reshape(nb_c, l_c, MLA_ROPE))
        y_mla_c = _mla_attention(
            (q, pl.BlockSpec((l_c, GROUP_W), lambda b, i: (b, 0))),
            (q, pl.BlockSpec((l_c, N_HEADS * MLA_ROPE), lambda b, i: (b, GROUP_W // (N_HEADS * MLA_ROPE)))),
            [((kv, pl.BlockSpec((l_c, GROUP_W), lambda b, i: (b, 0))),
              (proj, pl.BlockSpec((l_c, LANES), lambda b, i: (b, COL_KROPE // LANES))),
              (kv, pl.BlockSpec((l_c, GROUP_W), lambda b, i: (b, 1))))],
            nb=nb_c, lq=l_c, tq=l_c, name="mla_context")
        q_l = q[m_c:]
        qr_l = _axial_rope(q_l[:, GROUP_W:].reshape(nb_l, l_l, N_HEADS, MLA_ROPE).transpose(0, 2, 1, 3))
        qr_l = qr_l.transpose(0, 2, 1, 3).reshape(m_l, N_HEADS * MLA_ROPE)
        kr_l = _axial_rope(proj[m_c:, COL_KROPE:COL_KROPE + MLA_ROPE].reshape(nb_l, l_l, MLA_ROPE))
        kr_l = kr_l.reshape(m_l, MLA_ROPE)
        tq = 256
        nq = l_l // tq
        rb_l = m_c // l_l
        y_mla_l = _mla_attention(
            (q, pl.BlockSpec((tq, GROUP_W), lambda b, i: (m_c // tq + b * nq + i, 0))),
            (qr_l, pl.BlockSpec((tq, N_HEADS * MLA_ROPE), lambda b, i: (b * nq + i, 0))),
            [((kv_cache, pl.BlockSpec((past, GROUP_W), lambda b, i: (b, 0))),
              (cache_mla_krope, pl.BlockSpec((None, None, past, MLA_ROPE), lambda b, i: (b, layer, 0, 0))),
              (kv_cache, pl.BlockSpec((past, GROUP_W), lambda b, i: (b, 1)))),
             ((kv, pl.BlockSpec((l_l, GROUP_W), lambda b, i: (rb_l + b, 0))),
              (kr_l, pl.BlockSpec((l_l, MLA_ROPE), lambda b, i: (b, 0))),
              (kv, pl.BlockSpec((l_l, GROUP_W), lambda b, i: (rb_l + b, 1))))],
            nb=nb_l, lq=l_l, tq=tq, name="mla_latent")
        y_mla = jnp.concatenate([y_mla_c, y_mla_l], axis=0)

        lg = lg_all[layer]
        y_ret_c, s_c = _retention(proj, 0, nb_c, l_c, l_c, lg, ret_gn[layer], emit_state=True)
        y_ret_l = _retention(proj, m_c, nb_l, l_l, 256, lg, ret_gn[layer], state=state_ret, layer=layer)
        y_ret = jnp.concatenate([y_ret_c, y_ret_l], axis=0)
        new_s.append(s_c)

        yp = _out_matmul((y_hy, y_na, y_mla, y_ret), w_out_b[layer])
        x1, h_ffn, top_i, top_g, pos, cnt = _post_router(
            yp, x, norm_mix_post[layer], ga, norm_ffn_pre[layer], cf, sf, rw_hi[layer], rw_lo[layer],
            rb_p[layer:layer + 1], grp_fn(256))

        counts = cnt[0, :n_exp].astype(jnp.int32)
        y_rows, dest = _moe_experts(h_ffn, top_i[:, :TOP_K], pos[:, :TOP_K], counts,
                                    moe_w1, moe_b1, moe_w2, moe_b2, layer)
        ys = [y_rows.at[dest[:, kk]].get(mode="promise_in_bounds") for kk in range(TOP_K)]
        x = _combine(ys, top_g, x1, norm_ffn_post[layer], gf, grp_fn(128))

    return (x[:m_c].reshape(nb_c, l_c, d), x[m_c:].reshape(nb_l, l_l, d),
            jnp.stack(new_k, axis=1), jnp.stack(new_v, axis=1), jnp.stack(new_ckv, axis=1),
            jnp.stack(new_kr, axis=1), jnp.stack(new_s, axis=1))
```

```python
import functools
import math

import jax
import jax.numpy as jnp
from jax import lax
from jax.experimental import pallas as pl
from jax.experimental.pallas import tpu as pltpu

F32 = jnp.float32
BF16 = jnp.bfloat16
SDS = jax.ShapeDtypeStruct

EPS = 1e-6
GRID_W = 64
HEAD_DIM = 128
N_HEADS = 8
GROUP_W = N_HEADS * HEAD_DIM
HY_BANDS = 16
HY_TARGET = 1e-2
HY_FAST_DECAY = 0.3
HY_SLOW_DECAY = 1.5
NA_WR = 8
NA_WC = 16
MLA_Q_LORA = 768
MLA_KV_LORA = 512
MLA_ROPE = 64
MLA_NOPE = 128
MLA_SCALE = (MLA_NOPE + MLA_ROPE) ** -0.5
ROPE_BASE = 10000.0
TOP_K = 4
SWIGLU_LIMIT = 7.0
SWIGLU_ALPHA = 1.702
LANES = 128

COL_HY = 0
COL_NA = 3072
COL_RET = 6144
COL_CKV = 10240
COL_QC = 10752
COL_KROPE = 11520
N_PROJ = 11776

VMEM_LIMIT_MB = 56


def _cp(sem, vmem_mb=48):
    return pltpu.CompilerParams(dimension_semantics=sem, vmem_limit_bytes=min(vmem_mb, VMEM_LIMIT_MB) * 2**20)


def _dot(a, b):
    return jnp.dot(a, b, preferred_element_type=F32)


def _dot_nt(a, b):
    return lax.dot_general(a, b, (((1,), (1,)), ((), ())), preferred_element_type=F32)


def _dot_tn(a, b):
    return lax.dot_general(a, b, (((0,), (0,)), ((), ())), preferred_element_type=F32)


def _silu(x):
    return x * jax.nn.sigmoid(x)


def _mod_body(c_ref, w_ref, b_ref, o_ref):
    s = _silu(c_ref[...])
    o_ref[...] = _dot(s.astype(BF16), w_ref[...].astype(BF16)) + b_ref[...]


def _modulation(cv, w_mod, b_mod, layer):
    _, d, n = w_mod.shape
    tn = 512
    return pl.pallas_call(
        _mod_body,
        grid=(n // tn,),
        in_specs=[pl.BlockSpec((8, d), lambda j: (0, 0)),
                  pl.BlockSpec((None, d, tn), lambda j: (layer, 0, j)),
                  pl.BlockSpec((None, 1, tn), lambda j: (layer, 0, j))],
        out_specs=pl.BlockSpec((8, tn), lambda j: (0, j)),
        out_shape=SDS((8, n), F32),
        compiler_params=_cp(("arbitrary",), 40),
        name="modulation",
    )(cv, w_mod, b_mod.reshape(b_mod.shape[0], 1, n))


def _norm_matmul_body(*refs, do_norm, has_mod, emit_h):
    it = iter(refs)
    x_ref = next(it)
    g_ref = next(it) if do_norm else None
    sc_ref = next(it) if has_mod else None
    sh_ref = next(it) if has_mod else None
    w_ref = next(it)
    o_ref = next(it)
    h_ref = next(it) if emit_h else None
    hs_ref = next(it)

    @pl.when(pl.program_id(1) == 0)
    def _():
        h = x_ref[...].astype(F32)
        if do_norm:
            h = h * lax.rsqrt(jnp.mean(h * h, axis=-1, keepdims=True) + EPS) * g_ref[...]
        if has_mod:
            h = h * (1.0 + sc_ref[...]) + sh_ref[...]
        if emit_h:
            h_ref[...] = h
        hs_ref[...] = h.astype(BF16)

    o_ref[...] = _dot(hs_ref[...], w_ref[...].astype(BF16)).astype(o_ref.dtype)


def _norm_matmul(x, xcol, k, w, *, tm, tn, gain=None, mod=None, grp=None, emit_h=False,
                 out_dtype=F32, xrow0=0, m=None, name="norm_matmul"):
    m = x.shape[0] if m is None else m
    n = w.shape[1]
    do_norm = gain is not None
    has_mod = mod is not None
    in_specs = [pl.BlockSpec((tm, k), lambda i, j: (i + xrow0, xcol))]
    args = [x]
    if do_norm:
        in_specs.append(pl.BlockSpec((1, k), lambda i, j: (0, 0)))
        args.append(gain.reshape(1, k))
    if has_mod:
        for a in mod:
            in_specs.append(pl.BlockSpec((None, 1, k), lambda i, j: (grp(i), 0, 0)))
            args.append(a)
    in_specs.append(pl.BlockSpec((k, tn), lambda i, j: (0, j)))
    args.append(w)
    out_specs = [pl.BlockSpec((tm, tn), lambda i, j: (i, j))]
    out_shape = [SDS((m, n), out_dtype)]
    if emit_h:
        out_specs.append(pl.BlockSpec((tm, k), lambda i, j: (i, 0)))
        out_shape.append(SDS((m, k), F32))
    res = pl.pallas_call(
        functools.partial(_norm_matmul_body, do_norm=do_norm, has_mod=has_mod, emit_h=emit_h),
        grid=(m // tm, n // tn),
        in_specs=in_specs, out_specs=out_specs, out_shape=out_shape,
        scratch_shapes=[pltpu.VMEM((tm, k), BF16)],
        compiler_params=_cp(("arbitrary", "arbitrary"), 48),
        name=name,
    )(*args)
    return res if emit_h else res[0]


def _dft_tables(l):
    n = 2 * l
    f = jnp.arange(l, dtype=jnp.int32)
    ang = ((f[:, None] * f[None, :]) % n).astype(F32) * (2.0 * math.pi / n)
    fc = jnp.cos(ang)
    alt = jnp.where(f % 2 == 0, 1.0, -1.0).astype(F32)
    fs = jnp.where(f[:, None] == 0, alt[None, :], jnp.sin(ang))
    return fc.astype(BF16), fs.astype(BF16), fs.T.astype(BF16)


def _hy_spec_body(fc_ref, fs_ref, hf_ref, hb_ref, a_ref, b_ref, *, l):
    hf = hf_ref[...]
    row = lax.broadcasted_iota(jnp.int32, hf.shape, 0)
    hb = jnp.where(row == 0, 0.0, hb_ref[...])
    a = _dot(fc_ref[...], (hf + hb).astype(BF16))
    p1 = _dot(fs_ref[...], hf.astype(BF16))
    p2 = _dot(fs_ref[...], hb.astype(BF16))
    orow = lax.broadcasted_iota(jnp.int32, a.shape, 0)
    first = jnp.logical_and(pl.program_id(0) == 0, orow == 0)
    wgt = jnp.where(first, 0.5 / l, 1.0 / l)
    a_ref[...] = wgt * a
    b_ref[...] = wgt * jnp.where(first, p1 + p2, p1 - p2)


def _hy_spectra(fc, fs, hf, hb):
    l, c2 = hf.shape
    tf = min(l, 256)
    tc = 512
    return pl.pallas_call(
        functools.partial(_hy_spec_body, l=l),
        grid=(l // tf, c2 // tc),
        in_specs=[pl.BlockSpec((tf, l), lambda i, j: (i, 0)),
                  pl.BlockSpec((tf, l), lambda i, j: (i, 0)),
                  pl.BlockSpec((l, tc), lambda i, j: (0, j)),
                  pl.BlockSpec((l, tc), lambda i, j: (0, j))],
        out_specs=[pl.BlockSpec((tf, tc), lambda i, j: (i, j))] * 2,
        out_shape=[SDS((l, c2), F32)] * 2,
        compiler_params=_cp(("arbitrary", "arbitrary"), 40),
        name="hyena_filter_spectra",
    )(fc, fs, hf, hb)


def _short_conv(x, w, b):
    l = x.shape[0]
    row = lax.broadcasted_iota(jnp.int32, x.shape, 0)
    xm = jnp.where(row == 0, 0.0, pltpu.roll(x, 1, 0))
    xp = jnp.where(row == l - 1, 0.0, pltpu.roll(x, l - 1, 0))
    return xm * w[0:1, :] + x * w[1:2, :] + xp * w[2:3, :] + b


def _hy_conv_body(*refs, first_order, nf):
    it = iter(refs)
    z_ref = next(it)
    zw_ref = next(it) if first_order else None
    zb_ref = next(it) if first_order else None
    gt_ref, gw_ref, gb_ref, skip_ref = next(it), next(it), next(it), next(it)
    ah_ref, bh_ref, fc_ref, fs_ref, gc_ref, gs_ref = (next(it) for _ in range(6))
    o_ref = next(it)
    zf_scr, zb_scr, acc = next(it), next(it), next(it)
    fb = pl.program_id(2)

    @pl.when(fb == 0)
    def _():
        z = z_ref[...].astype(F32)
        if first_order:
            z = _short_conv(z, zw_ref[...], zb_ref[...])
        zf_scr[...] = z
        zb_scr[...] = z.astype(BF16)
        acc[...] = jnp.zeros_like(acc)

    zb = zb_scr[...]
    uc = _dot(fc_ref[...], zb)
    us = _dot(fs_ref[...], zb)
    ah = ah_ref[...]
    bh = bh_ref[...]
    row = lax.broadcasted_iota(jnp.int32, uc.shape, 0)
    first = jnp.logical_and(fb == 0, row == 0)
    usb = us * bh
    yc = uc * ah - jnp.where(first, 0.0, usb)
    ys = jnp.where(first, usb, uc * bh + us * ah)
    acc[...] += _dot(gc_ref[...], yc.astype(BF16)) + _dot(gs_ref[...], ys.astype(BF16))

    @pl.when(fb == nf - 1)
    def _():
        gate = _short_conv(gt_ref[...].astype(F32), gw_ref[...], gb_ref[...])
        o_ref[...] = (gate * (acc[...] + zf_scr[...] * skip_ref[...])).astype(o_ref.dtype)


def _hy_conv(z, gate_col, proj, rb, conv_w, conv_b, skip, ah, bh, spec_col0, tables, *,
             nb, l, z_col, out_dtype):
    fc, fs, gs = tables
    first_order = z_col is not None
    tc = 512 if l > 512 else GROUP_W
    tf = min(l, 256)
    nf = l // tf

    def proj_specs(col):
        return [pl.BlockSpec((l, tc), lambda b, c, f: (b + rb, col // tc + c)),
                pl.BlockSpec((3, tc), lambda b, c, f: (0, (col - COL_HY) // tc + c)),
                pl.BlockSpec((1, tc), lambda b, c, f: (0, (col - COL_HY) // tc + c))]

    if first_order:
        in_specs = proj_specs(z_col)
        args = [proj, conv_w, conv_b]
    else:
        in_specs = [pl.BlockSpec((l, tc), lambda b, c, f: (b, c))]
        args = [z]
    in_specs += proj_specs(gate_col)
    args += [proj, conv_w, conv_b]
    in_specs += [pl.BlockSpec((1, tc), lambda b, c, f: (0, c)),
                 pl.BlockSpec((tf, tc), lambda b, c, f: (f, spec_col0 // tc + c)),
                 pl.BlockSpec((tf, tc), lambda b, c, f: (f, spec_col0 // tc + c)),
                 pl.BlockSpec((tf, l), lambda b, c, f: (f, 0)),
                 pl.BlockSpec((tf, l), lambda b, c, f: (f, 0)),
                 pl.BlockSpec((l, tf), lambda b, c, f: (0, f)),
                 pl.BlockSpec((l, tf), lambda b, c, f: (0, f))]
    args += [skip, ah, bh, fc, fs, fc, gs]
    return pl.pallas_call(
        functools.partial(_hy_conv_body, first_order=first_order, nf=nf),
        grid=(nb, GROUP_W // tc, nf),
        in_specs=in_specs,
        out_specs=pl.BlockSpec((l, tc), lambda b, c, f: (b, c)),
        out_shape=SDS((nb * l, GROUP_W), out_dtype),
        scratch_shapes=[pltpu.VMEM((l, tc), F32), pltpu.VMEM((l, tc), BF16), pltpu.VMEM((l, tc), F32)],
        compiler_params=_cp(("arbitrary", "arbitrary", "arbitrary"), 52),
        name="hyena_conv",
    )(*args)


def _hyena_filters(l, w1, b1, w2, b2, w3):
    hp = lax.Precision.HIGHEST
    pos = jnp.arange(l, dtype=F32)
    t = pos / l
    bands = jnp.linspace(1e-4, HY_BANDS - 1, HY_BANDS, dtype=F32)
    ang = (2.0 * math.pi * t)[:, None] * bands[None, :]
    z = jnp.concatenate([t[:, None], jnp.cos(ang), -jnp.sin(ang)], axis=-1)
    h = jnp.sin(jnp.dot(z, w1, precision=hp) + b1)
    h = jnp.sin(jnp.dot(h, w2, precision=hp) + b2)
    h = jnp.dot(h, w3, precision=hp)
    deltas = jnp.abs(jnp.linspace(math.log(HY_TARGET) / HY_FAST_DECAY,
                                  math.log(HY_TARGET) / HY_SLOW_DECAY, GROUP_W, dtype=F32))
    window = jnp.exp(-t[:, None] * deltas[None, :])
    filt = h.reshape(l, 2, 2, GROUP_W) * window[:, None, None, :]
    return filt[:, :, 0, :].reshape(l, 2 * GROUP_W), filt[:, :, 1, :].reshape(l, 2 * GROUP_W)


def _hyena(proj, row0, nb, l, lw, tables):
    fc, fs, gs = tables
    hf, hb = _hyena_filters(l, lw["hy_filt_w1"], lw["hy_filt_b1"], lw["hy_filt_w2"], lw["hy_filt_b2"],
                            lw["hy_filt_w3"])
    ah, bh = _hy_spectra(fc, fs, hf, hb)
    cw = lw["hy_conv_w"]
    cb = lw["hy_conv_b"].reshape(1, -1)
    skip = lw["hy_bias"]
    rb = row0 // l
    z1 = _hy_conv(None, COL_HY, proj, rb, cw, cb, skip[0:1], ah, bh, 0, tables,
                  nb=nb, l=l, z_col=COL_HY + 2 * GROUP_W, out_dtype=F32)
    return _hy_conv(z1, COL_HY + GROUP_W, proj, rb, cw, cb, skip[1:2], ah, bh, GROUP_W, tables,
                    nb=nb, l=l, z_col=None, out_dtype=BF16)


def _softmax_attend(scores, values):
    m = functools.reduce(jnp.maximum, [jnp.max(s, axis=-1, keepdims=True) for s in scores])
    ps = [jnp.exp(s - m) for s in scores]
    den = functools.reduce(jnp.add, [jnp.sum(p, axis=-1, keepdims=True) for p in ps])
    o = functools.reduce(jnp.add, [_dot(p.astype(BF16), v) for p, v in zip(ps, values)])
    return o / den


def _na_ctx_body(q_ref, k_ref, v_ref, o_ref, kc_ref, vc_ref):
    scale = HEAD_DIM ** -0.5
    for h in range(N_HEADS):
        hs = slice(h * HEAD_DIM, (h + 1) * HEAD_DIM)
        k = k_ref[:, hs]
        v = v_ref[:, hs]
        kc_ref[h] = k
        vc_ref[h] = v
        s = _dot_nt(q_ref[:, hs].astype(BF16), k.astype(BF16)) * scale
        o_ref[:, hs] = _softmax_attend([s], [v.astype(BF16)]).astype(o_ref.dtype)


def _na_context(proj, nb, l):
    c0 = COL_NA // GROUP_W
    return pl.pallas_call(
        _na_ctx_body,
        grid=(nb,),
        in_specs=[pl.BlockSpec((l, GROUP_W), lambda b: (b, c0)),
                  pl.BlockSpec((l, GROUP_W), lambda b: (b, c0 + 1)),
                  pl.BlockSpec((l, GROUP_W), lambda b: (b, c0 + 2))],
        out_specs=[pl.BlockSpec((l, GROUP_W), lambda b: (b, 0)),
                   pl.BlockSpec((None, N_HEADS, l, HEAD_DIM), lambda b: (b, 0, 0, 0)),
                   pl.BlockSpec((None, N_HEADS, l, HEAD_DIM), lambda b: (b, 0, 0, 0))],
        out_shape=[SDS((nb * l, GROUP_W), BF16),
                   SDS((nb, N_HEADS, l, HEAD_DIM), F32),
                   SDS((nb, N_HEADS, l, HEAD_DIM), F32)],
        compiler_params=_cp(("arbitrary",), 32),
        name="na_context",
    )(proj, proj, proj)


def _na_bias_table(rpb):
    cidx = jnp.arange(GRID_W)
    col_start = jnp.clip(cidx - NA_WC // 2, 0, GRID_W - NA_WC)
    col_in = (cidx[None, :] >= col_start[:, None]) & (cidx[None, :] < col_start[:, None] + NA_WC)
    col_off = jnp.clip(cidx[None, :] - cidx[:, None], -(NA_WC - 1), NA_WC - 1) + (NA_WC - 1)
    t = rpb[:, :, col_off].astype(F32)
    t = jnp.where(col_in[None, None], t, -jnp.inf)
    return t.transpose(0, 2, 1, 3).reshape(rpb.shape[0], GRID_W, -1)


def _na_lat_body(q_ref, k_ref, v_ref, kc_ref, vc_ref, bt_ref, o_ref, *, rows):
    scale = HEAD_DIM ** -0.5
    wr = min(NA_WR, rows)
    kcx = kc_ref[...].astype(BF16)
    vcx = vc_ref[...].astype(BF16)
    nloc = wr * GRID_W
    bt = bt_ref[...]
    for r in range(rows):
        rs = min(max(r - wr // 2, 0), rows - wr)
        off = rs - r + (NA_WR - 1)
        q = q_ref[r * GRID_W:(r + 1) * GRID_W, :].astype(BF16)
        kl = k_ref[rs * GRID_W:rs * GRID_W + nloc, :].astype(BF16)
        vl = v_ref[rs * GRID_W:rs * GRID_W + nloc, :].astype(BF16)
        s_loc = _dot_nt(q, kl) * scale + bt[:, off * GRID_W:off * GRID_W + nloc]
        s_ctx = _dot_nt(q, kcx) * scale
        o_ref[r * GRID_W:(r + 1) * GRID_W, :] = _softmax_attend([s_loc, s_ctx], [vl, vcx]).astype(o_ref.dtype)


def _na_latent(proj, row0, nb, l, cache_k, cache_v, layer, bias_table):
    rows = l // GRID_W
    rb = row0 // l
    c0 = COL_NA // HEAD_DIM
    past = cache_k.shape[3]
    cspec = pl.BlockSpec((None, None, None, past, HEAD_DIM), lambda b, h: (b, layer, h, 0, 0))
    return pl.pallas_call(
        functools.partial(_na_lat_body, rows=rows),
        grid=(nb, N_HEADS),
        in_specs=[pl.BlockSpec((l, HEAD_DIM), lambda b, h: (b + rb, c0 + h)),
                  pl.BlockSpec((l, HEAD_DIM), lambda b, h: (b + rb, c0 + N_HEADS + h)),
                  pl.BlockSpec((l, HEAD_DIM), lambda b, h: (b + rb, c0 + 2 * N_HEADS + h)),
                  cspec, cspec,
                  pl.BlockSpec((None, GRID_W, bias_table.shape[2]), lambda b, h: (h, 0, 0))],
        out_specs=pl.BlockSpec((l, HEAD_DIM), lambda b, h: (b, h)),
        out_shape=SDS((nb * l, GROUP_W), BF16),
        compiler_params=_cp(("arbitrary", "arbitrary"), 32),
        name="na_latent",
    )(proj, proj, proj, cache_k, cache_v, bias_table)


def _mla_attn_body(*refs, nseg):
    qn_ref, qr_ref = refs[0], refs[1]
    segs = [refs[2 + 3 * i: 5 + 3 * i] for i in range(nseg)]
    o_ref = refs[2 + 3 * nseg]
    krs = [kr_ref[...][:, :MLA_ROPE].astype(BF16) for _, kr_ref, _ in segs]
    qr_all = qr_ref[...].astype(BF16)
    for h in range(N_HEADS):
        hs = slice(h * HEAD_DIM, (h + 1) * HEAD_DIM)
        qn = qn_ref[:, hs].astype(BF16)
        qr = qr_all[:, h * MLA_ROPE:(h + 1) * MLA_ROPE]
        scores = [(_dot_nt(qn, kn_ref[:, hs].astype(BF16)) + _dot_nt(qr, kr)) * MLA_SCALE
                  for (kn_ref, _, _), kr in zip(segs, krs)]
        vals = [v_ref[:, hs].astype(BF16) for _, _, v_ref in segs]
        o_ref[:, hs] = _softmax_attend(scores, vals).astype(o_ref.dtype)


def _mla_attention(qn, qr, segs, *, nb, lq, tq, name):
    arrays = [qn[0], qr[0]]
    specs = [qn[1], qr[1]]
    for s in segs:
        for a, sp in s:
            arrays.append(a)
            specs.append(sp)
    return pl.pallas_call(
        functools.partial(_mla_attn_body, nseg=len(segs)),
        grid=(nb, lq // tq),
        in_specs=specs,
        out_specs=pl.BlockSpec((tq, GROUP_W), lambda b, i: (b * (lq // tq) + i, 0)),
        out_shape=SDS((nb * lq, GROUP_W), BF16),
        compiler_params=_cp(("arbitrary", "arbitrary"), 48),
        name=name,
    )(*arrays)


def _axial_rope(x):
    l = x.shape[-2]
    pos = jnp.arange(l)
    half = MLA_ROPE // 2
    freqs = jnp.power(ROPE_BASE, -jnp.arange(0, half, 2, dtype=F32) / half)

    def rot(xa, p):
        ang = p.astype(F32)[:, None] * freqs[None, :]
        cos, sin = jnp.cos(ang), jnp.sin(ang)
        a1, a2 = jnp.split(xa, 2, axis=-1)
        return jnp.concatenate([a1 * cos - a2 * sin, a2 * cos + a1 * sin], axis=-1)

    return jnp.concatenate([rot(x[..., :half], pos // GRID_W), rot(x[..., half:], pos % GRID_W)], axis=-1)


def _ret_body(*refs, l, tq, has_state, emit_state):
    it = iter(refs)
    lg_ref = next(it)
    q_ref, k_ref, v_ref, g_ref, gn_ref = (next(it) for _ in range(5))
    s0_ref = next(it) if has_state else None
    o_ref = next(it)
    so_ref = next(it) if emit_state else None
    t0 = pl.program_id(1) * tq
    ti = lax.broadcasted_iota(jnp.int32, (tq, l), 0) + t0
    si = lax.broadcasted_iota(jnp.int32, (tq, l), 1)
    d = (ti - si).astype(F32)
    tcol = (lax.broadcasted_iota(jnp.int32, (tq, 1), 0) + t0).astype(F32)
    scol = lax.broadcasted_iota(jnp.int32, (l, 1), 0).astype(F32)
    kscale = HEAD_DIM ** -0.5
    for h in range(N_HEADS):
        hs = slice(h * HEAD_DIM, (h + 1) * HEAD_DIM)
        lgf = lg_ref[0, h]
        lgb = lg_ref[1, h]
        q = q_ref[:, hs]
        k = k_ref[:, hs] * kscale
        v = v_ref[:, hs].astype(BF16)
        s = _dot_nt(q.astype(BF16), k.astype(BF16))
        w = jnp.exp(jnp.where(d > 0, lgf, -lgb) * d)
        w = jnp.where(d == 0, 2.0, w)
        o = _dot((s * w).astype(BF16), v)
        if has_state:
            qf = q * jnp.exp(lgf * (tcol + 1.0))
            qb = q * jnp.exp(lgb * (l - tcol))
            o = o + _dot(qf.astype(BF16), s0_ref[0, h].astype(BF16)) + _dot(qb.astype(BF16), s0_ref[1, h].astype(BF16))
        o = o * lax.rsqrt(jnp.mean(o * o, axis=-1, keepdims=True) + EPS) * gn_ref[:, hs]
        o_ref[:, hs] = (o * _silu(g_ref[:, hs])).astype(o_ref.dtype)
        if emit_state:
            kf = k * jnp.exp(lgf * (l - 1.0 - scol))
            kb = k * jnp.exp(lgb * scol)
            so_ref[0, h] = _dot_tn(kf.astype(BF16), v)
            so_ref[1, h] = _dot_tn(kb.astype(BF16), v)


def _retention(proj, row0, nb, l, tq, lg, gn, state=None, layer=0, emit_state=False):
    assert not emit_state or tq == l
    nq = l // tq
    rbq = row0 // tq
    rbl = row0 // l
    c0 = COL_RET // GROUP_W
    in_specs = [pl.BlockSpec(memory_space=pltpu.SMEM),
                pl.BlockSpec((tq, GROUP_W), lambda b, i: (rbq + b * nq + i, c0)),
                pl.BlockSpec((l, GROUP_W), lambda b, i: (rbl + b, c0 + 1)),
                pl.BlockSpec((l, GROUP_W), lambda b, i: (rbl + b, c0 + 2)),
                pl.BlockSpec((tq, GROUP_W), lambda b, i: (rbq + b * nq + i, c0 + 3)),
                pl.BlockSpec((1, GROUP_W), lambda b, i: (0, 0))]
    args = [lg, proj, proj, proj, proj, gn.reshape(1, GROUP_W)]
    if state is not None:
        in_specs.append(pl.BlockSpec((None, None, 2, N_HEADS, HEAD_DIM, HEAD_DIM),
                                     lambda b, i: (b, layer, 0, 0, 0, 0)))
        args.append(state)
    out_specs = [pl.BlockSpec((tq, GROUP_W), lambda b, i: (b * nq + i, 0))]
    out_shape = [SDS((nb * l, GROUP_W), BF16)]
    if emit_state:
        out_specs.append(pl.BlockSpec((None, 2, N_HEADS, HEAD_DIM, HEAD_DIM), lambda b, i: (b, 0, 0, 0, 0)))
        out_shape.append(SDS((nb, 2, N_HEADS, HEAD_DIM, HEAD_DIM), F32))
    res = pl.pallas_call(
        functools.partial(_ret_body, l=l, tq=tq, has_state=state is not None, emit_state=emit_state),
        grid=(nb, nq),
        in_specs=in_specs, out_specs=out_specs, out_shape=out_shape,
        compiler_params=_cp(("arbitrary", "arbitrary"), 48),
        name="retention",
    )(*args)
    return res if emit_state else res[0]


def _out_matmul_body(*refs, nc):
    ctx, lat, w_ref, o_ref = refs[0:4], refs[4:8], refs[8], refs[9]

    def project(ys):
        acc = _dot(ys[0][...], w_ref[0 * GROUP_W:1 * GROUP_W, :])
        acc += _dot(ys[1][...], w_ref[1 * GROUP_W:2 * GROUP_W, :])
        acc += _dot(ys[2][...], w_ref[2 * GROUP_W:3 * GROUP_W, :])
        acc += _dot(ys[3][...], w_ref[3 * GROUP_W:4 * GROUP_W, :])
        o_ref[...] = acc

    @pl.when(pl.program_id(1) < nc)
    def _():
        project(ctx)

    @pl.when(pl.program_id(1) >= nc)
    def _():
        project(lat)


def _out_matmul(ys_ctx, ys_lat, w_out, *, tm=512, tn=1024):
    m_c = ys_ctx[0].shape[0]
    m = m_c + ys_lat[0].shape[0]
    nc = m_c // tm
    d = w_out.shape[1]
    cspec = pl.BlockSpec((tm, GROUP_W), lambda j, i: (jnp.minimum(i, nc - 1), 0))
    lspec = pl.BlockSpec((tm, GROUP_W), lambda j, i: (jnp.maximum(i - nc, 0), 0))
    return pl.pallas_call(
        functools.partial(_out_matmul_body, nc=nc),
        grid=(d // tn, m // tm),
        in_specs=[cspec] * 4 + [lspec] * 4 + [pl.BlockSpec((4 * GROUP_W, tn), lambda j, i: (0, j))],
        out_specs=pl.BlockSpec((tm, tn), lambda j, i: (i, j)),
        out_shape=SDS((m, d), F32),
        compiler_params=_cp(("arbitrary", "arbitrary"), 48),
        name="out_proj",
    )(*ys_ctx, *ys_lat, w_out)


def _topk(logits):
    lane = lax.broadcasted_iota(jnp.int32, logits.shape, 1)
    work = logits
    idx_out = jnp.zeros(logits.shape, jnp.int32)
    val_out = jnp.full(logits.shape, -jnp.inf, F32)
    chosen = jnp.zeros(logits.shape, F32)
    idxs = []
    for k in range(TOP_K):
        m = jnp.max(work, axis=-1, keepdims=True)
        idx = jnp.min(jnp.where(work == m, lane, LANES), axis=-1, keepdims=True)
        idx_out = jnp.where(lane == k, idx, idx_out)
        val_out = jnp.where(lane == k, m, val_out)
        chosen = jnp.where(lane == idx, 1.0, chosen)
        work = jnp.where(lane == idx, -jnp.inf, work)
        idxs.append(idx)
    e = jnp.exp(val_out - jnp.max(val_out, axis=-1, keepdims=True))
    return idx_out, e / jnp.sum(e, axis=-1, keepdims=True), chosen, idxs


def _post_router_body(yp_ref, x_ref, gpost_ref, ga_ref, gpre_ref, cf_ref, sf_ref, rwh_ref, rwl_ref, rb_ref,
                      x1_ref, h_ref, ti_ref, tg_ref, pos_ref, cnt_ref, carry):
    @pl.when(pl.program_id(0) == 0)
    def _():
        carry[...] = jnp.zeros_like(carry)

    y = yp_ref[...]
    y = y * lax.rsqrt(jnp.mean(y * y, axis=-1, keepdims=True) + EPS) * gpost_ref[...]
    x1 = x_ref[...] + ga_ref[...] * y
    x1_ref[...] = x1
    h = x1 * lax.rsqrt(jnp.mean(x1 * x1, axis=-1, keepdims=True) + EPS) * gpre_ref[...]
    h = h * (1.0 + cf_ref[...]) + sf_ref[...]
    h_ref[...] = h.astype(h_ref.dtype)
    h_hi = h.astype(BF16)
    h_lo = (h - h_hi.astype(F32)).astype(BF16)
    logits = _dot(h_hi, rwh_ref[...]) + _dot(h_lo, rwh_ref[...]) + _dot(h_hi, rwl_ref[...]) + rb_ref[...]
    idx_out, gates, chosen, idxs = _topk(logits)
    ti_ref[...] = idx_out
    tg_ref[...] = gates
    tm = chosen.shape[0]
    r = lax.broadcasted_iota(jnp.int32, (tm, tm), 0)
    c = lax.broadcasted_iota(jnp.int32, (tm, tm), 1)
    before = _dot(jnp.where(c < r, 1.0, 0.0).astype(BF16), chosen.astype(BF16)) + carry[0:1, :]
    lane = lax.broadcasted_iota(jnp.int32, chosen.shape, 1)
    pos = jnp.zeros(chosen.shape, jnp.int32)
    for k, idx in enumerate(idxs):
        pk = jnp.sum(jnp.where(lane == idx, before, 0.0), axis=-1, keepdims=True)
        pos = jnp.where(lane == k, pk.astype(jnp.int32), pos)
    pos_ref[...] = pos
    carry[...] = carry[...] + jnp.sum(chosen, axis=0, keepdims=True)
    cnt_ref[...] = carry[...]


def _post_router(yp, x, gpost, ga, gpre, cf, sf, rw_hi, rw_lo, rb, grp, *, tm=256):
    m, d = x.shape
    row = pl.BlockSpec((1, d), lambda i: (0, 0))
    modspec = pl.BlockSpec((None, 1, d), lambda i: (grp(i), 0, 0))
    full = pl.BlockSpec((tm, d), lambda i: (i, 0))
    small = pl.BlockSpec((tm, LANES), lambda i: (i, 0))
    rwspec = pl.BlockSpec((d, LANES), lambda i: (0, 0))
    return pl.pallas_call(
        _post_router_body,
        grid=(m // tm,),
        in_specs=[full, full, row, modspec, row, modspec, modspec, rwspec, rwspec,
                  pl.BlockSpec((1, LANES), lambda i: (0, 0))],
        out_specs=[full, full, small, small, small, pl.BlockSpec((8, LANES), lambda i: (0, 0))],
        out_shape=[SDS((m, d), F32), SDS((m, d), BF16), SDS((m, LANES), jnp.int32), SDS((m, LANES), F32),
                   SDS((m, LANES), jnp.int32), SDS((8, LANES), F32)],
        scratch_shapes=[pltpu.VMEM((8, LANES), F32)],
        compiler_params=_cp(("arbitrary",), 52),
        name="post_norm_router",
    )(yp, x, gpost.reshape(1, d), ga, gpre.reshape(1, d), cf, sf, rw_hi, rw_lo, rb)


def _deinterleave(hh):
    tm, tn = hh.shape
    lane = lax.broadcasted_iota(jnp.int32, (tm, LANES), 1)
    half = LANES // 2
    idx = jnp.where(lane < half, 2 * lane, 2 * (lane - half) + 1)
    ev, od = [], []
    for c in range(tn // (2 * LANES)):
        g0 = jnp.take_along_axis(hh[:, (2 * c) * LANES:(2 * c + 1) * LANES], idx, axis=1)
        g1 = jnp.take_along_axis(hh[:, (2 * c + 1) * LANES:(2 * c + 2) * LANES], idx, axis=1)
        ev.append(jnp.where(lane < half, g0, pltpu.roll(g1, half, 1)))
        od.append(jnp.where(lane < half, pltpu.roll(g0, half, 1), g1))
    return jnp.concatenate(ev, axis=1), jnp.concatenate(od, axis=1)


def _cast_weights(w_ref, wb_ref, chunk=256):
    k = w_ref.shape[0]

    def step(c, carry):
        r = pl.multiple_of(c * chunk, chunk)
        wb_ref[pl.ds(r, chunk), :] = w_ref[pl.ds(r, chunk), :].astype(BF16)
        return carry

    lax.fori_loop(0, k // chunk, step, 0)


ROW_BLOCK = 512
ROW_PAD = 256


def _swiglu(hh):
    glu, lin = _deinterleave(hh)
    glu = jnp.minimum(glu, SWIGLU_LIMIT)
    lin = jnp.clip(lin, -SWIGLU_LIMIT, SWIGLU_LIMIT)
    return glu * jax.nn.sigmoid(SWIGLU_ALPHA * glu) * (lin + 1.0)


def _expert_body(n5_ref, tl_ref, r0_ref, lf_ref, x_hbm, w_ref, b_ref, o_hbm, wb, xbuf, obuf, xtail, otail,
                 sem_in, sem_out, sem_tail, *, swiglu, tn_out):
    e = pl.program_id(0)
    n5 = n5_ref[e]
    tl = tl_ref[e]
    r0 = r0_ref[e]
    col0 = pl.multiple_of(pl.program_id(1) * tn_out, tn_out)

    def x_dma(b, slot):
        row = pl.multiple_of(r0 + b * ROW_BLOCK, ROW_PAD)
        return pltpu.make_async_copy(x_hbm.at[pl.ds(row, ROW_BLOCK), :], xbuf.at[slot], sem_in.at[slot])

    def o_dma(b, slot):
        row = pl.multiple_of(r0 + b * ROW_BLOCK, ROW_PAD)
        return pltpu.make_async_copy(obuf.at[slot], o_hbm.at[pl.ds(row, ROW_BLOCK), pl.ds(col0, tn_out)],
                                     sem_out.at[slot])

    trow = pl.multiple_of(r0 + n5 * ROW_BLOCK, ROW_PAD)
    tail_in = pltpu.make_async_copy(x_hbm.at[pl.ds(trow, ROW_PAD), :], xtail, sem_tail.at[0])
    tail_out = pltpu.make_async_copy(otail, o_hbm.at[pl.ds(trow, ROW_PAD), pl.ds(col0, tn_out)], sem_tail.at[1])

    def compute(x):
        hh = _dot(x, wb[...]) + b_ref[...]
        return _swiglu(hh) if swiglu else hh

    @pl.when(n5 + tl > 0)
    def _():
        @pl.when(n5 > 0)
        def _():
            x_dma(0, 0).start()

        @pl.when(tl == 1)
        def _():
            tail_in.start()

        _cast_weights(w_ref, wb)

        def loop(b, carry):
            slot = lax.rem(b, 2)
            x_dma(b, slot).wait()

            @pl.when(b + 1 < n5)
            def _():
                x_dma(b + 1, 1 - slot).start()

            @pl.when(b >= 2)
            def _():
                o_dma(b - 2, slot).wait()

            obuf[slot] = compute(xbuf[slot]).astype(obuf.dtype)
            o_dma(b, slot).start()
            return carry

        lax.fori_loop(0, n5, loop, 0)

        @pl.when(n5 >= 2)
        def _():
            o_dma(n5 - 2, lax.rem(n5, 2)).wait()

        @pl.when(n5 >= 1)
        def _():
            o_dma(n5 - 1, lax.rem(n5 + 1, 2)).wait()

        @pl.when(tl == 1)
        def _():
            tail_in.wait()
            otail[...] = compute(xtail[...]).astype(otail.dtype)
            tail_out.start()
            tail_out.wait()

    @pl.when(e == pl.num_programs(0) - 1)
    def _():
        otail[...] = jnp.zeros_like(otail)

        def z_dma(c):
            row = pl.multiple_of(lf_ref[0] + c * ROW_PAD, ROW_PAD)
            return pltpu.make_async_copy(otail, o_hbm.at[pl.ds(row, ROW_PAD), pl.ds(col0, tn_out)], sem_tail.at[1])

        def start(c, carry):
            z_dma(c).start()
            return carry

        def wait(c, carry):
            z_dma(c).wait()
            return carry

        lax.fori_loop(0, lf_ref[1], start, 0)
        lax.fori_loop(0, lf_ref[1], wait, 0)


def _expert_matmul(sched, x, w, b, layer, *, tn, swiglu, out_dtype, vmem_mb, name):
    n_rows, k = x.shape
    n_exp, _, n = w.shape[1:]
    tn_out = tn // 2 if swiglu else tn
    n_out = n // 2 if swiglu else n
    return pl.pallas_call(
        functools.partial(_expert_body, swiglu=swiglu, tn_out=tn_out),
        grid_spec=pltpu.PrefetchScalarGridSpec(
            num_scalar_prefetch=4,
            grid=(n_exp, n // tn),
            in_specs=[pl.BlockSpec(memory_space=pl.ANY),
                      pl.BlockSpec((None, None, k, tn), lambda e, j, n5, tl, r0, lf: (layer, e, 0, j)),
                      pl.BlockSpec((None, None, 1, tn), lambda e, j, n5, tl, r0, lf: (layer, e, 0, j))],
            out_specs=pl.BlockSpec(memory_space=pl.ANY),
            scratch_shapes=[pltpu.VMEM((k, tn), BF16),
                            pltpu.VMEM((2, ROW_BLOCK, k), x.dtype),
                            pltpu.VMEM((2, ROW_BLOCK, tn_out), out_dtype),
                            pltpu.VMEM((ROW_PAD, k), x.dtype),
                            pltpu.VMEM((ROW_PAD, tn_out), out_dtype),
                            pltpu.SemaphoreType.DMA((2,)),
                            pltpu.SemaphoreType.DMA((2,)),
                            pltpu.SemaphoreType.DMA((2,))]),
        out_shape=SDS((n_rows, n_out), out_dtype),
        compiler_params=_cp(("arbitrary", "arbitrary"), vmem_mb),
        name=name,
    )(*sched, x, w, b.reshape(b.shape[0], n_exp, 1, n))


def _moe_experts(h, top_idx, pos, counts, w1, b1, w2, b2, layer, *, tn1=1024, tn2=1024):
    m, d = h.shape
    n_exp = w1.shape[1]
    n_assign = m * TOP_K
    flat_e = top_idx.reshape(-1)
    padded = (counts + ROW_PAD - 1) // ROW_PAD * ROW_PAD
    row0 = (jnp.cumsum(padded) - padded).astype(jnp.int32)
    dest = (row0[flat_e] + pos.reshape(-1)).astype(jnp.int32)
    n_rows = n_assign + n_exp * ROW_PAD
    tok = jnp.arange(n_assign, dtype=jnp.int32) // TOP_K
    row_tok = jnp.zeros((n_rows,), jnp.int32).at[dest].set(tok, mode="promise_in_bounds", unique_indices=True)
    xs = h.at[row_tok].get(mode="promise_in_bounds")
    total = jnp.sum(padded).astype(jnp.int32)
    leftover = jnp.stack([total, (n_rows - total) // ROW_PAD])
    sched = ((padded // ROW_BLOCK).astype(jnp.int32), ((padded // ROW_PAD) % 2).astype(jnp.int32), row0, leftover)
    act = _expert_matmul(sched, xs, w1, b1, layer, tn=tn1, swiglu=True, out_dtype=BF16, vmem_mb=56,
                         name="moe_w1")
    y_rows = _expert_matmul(sched, act, w2, b2, layer, tn=tn2, swiglu=False, out_dtype=F32, vmem_mb=48,
                            name="moe_w2")
    return y_rows, dest


TOKEN_TILE = (32, LANES)


def _gather_combine_body(dest_ref, y_hbm, g_ref, x_ref, gp_ref, gf_ref, o_ref, ybuf, sem, *, tm, n_blk, d):
    i = pl.program_id(0)

    def issue(blk, slot):
        base = blk * (tm * TOP_K)

        def body(t, carry):
            for k in range(TOP_K):
                row = dest_ref[base + t * TOP_K + k]
                pltpu.make_async_copy(y_hbm.at[row], ybuf.at[slot, k, t], sem.at[slot]).start()
            return carry

        lax.fori_loop(0, tm, body, 0)

    def wait_all(slot):
        def body(t, carry):
            for k in range(TOP_K):
                pltpu.make_async_copy(y_hbm.at[0], ybuf.at[slot, k, t], sem.at[slot]).wait()
            return carry

        lax.fori_loop(0, tm, body, 0)

    @pl.when(i == 0)
    def _():
        issue(0, 0)

    slot = lax.rem(i, 2)

    @pl.when(i + 1 < n_blk)
    def _():
        issue(i + 1, 1 - slot)

    wait_all(slot)
    y = (ybuf[slot, 0] * g_ref[0] + ybuf[slot, 1] * g_ref[1]) + (ybuf[slot, 2] * g_ref[2] + ybuf[slot, 3] * g_ref[3])
    ss = jnp.sum(jnp.sum(y * y, axis=2, keepdims=True), axis=1, keepdims=True)
    inv = lax.rsqrt(ss / d + EPS)
    o_ref[...] = x_ref[...] + gf_ref[...] * (y * inv * gp_ref[...])


def _gather_combine(y_rows, dest, top_gate, x, gpost, gf, grp, *, tm=128):
    m, d = x.shape
    n_blk = m // tm
    tile = TOKEN_TILE
    gates = jnp.broadcast_to(top_gate[:, :TOP_K].T[:, :, None, None], (TOP_K, m, 1, LANES))
    blk = pl.BlockSpec((tm,) + tile, lambda i, dref: (i, 0, 0))
    out = pl.pallas_call(
        functools.partial(_gather_combine_body, tm=tm, n_blk=n_blk, d=d),
        grid_spec=pltpu.PrefetchScalarGridSpec(
            num_scalar_prefetch=1,
            grid=(n_blk,),
            in_specs=[pl.BlockSpec(memory_space=pl.ANY),
                      pl.BlockSpec((TOP_K, tm, 1, LANES), lambda i, dref: (0, i, 0, 0)),
                      blk,
                      pl.BlockSpec((1,) + tile, lambda i, dref: (0, 0, 0)),
                      pl.BlockSpec((None, 1) + tile, lambda i, dref: (grp(i), 0, 0, 0))],
            out_specs=blk,
            scratch_shapes=[pltpu.VMEM((2, TOP_K, tm) + tile, F32), pltpu.SemaphoreType.DMA((2,))]),
        out_shape=SDS((m,) + tile, F32),
        compiler_params=_cp(("arbitrary",), 40),
        name="moe_gather_combine",
    )(dest, y_rows.reshape((-1,) + tile), gates, x.reshape((m,) + tile), gpost.reshape((1,) + tile),
      gf.reshape((gf.shape[0], 1) + tile))
    return out.reshape(m, d)


def _prep_w_in(w_in):
    hy_na = 6 * GROUP_W
    mla0 = hy_na
    ckv0 = mla0 + MLA_Q_LORA
    kr0 = ckv0 + MLA_KV_LORA
    ret0 = kr0 + MLA_ROPE
    pad = jnp.zeros(w_in.shape[:2] + (N_PROJ - COL_KROPE - MLA_ROPE,), w_in.dtype)
    parts = [w_in[..., :hy_na], w_in[..., ret0:], w_in[..., ckv0:kr0], w_in[..., mla0:ckv0],
             w_in[..., kr0:ret0], pad]
    return jnp.concatenate(parts, axis=-1).astype(BF16)


def _prep_heads(w, d_a, d_b):
    lead = w.shape[:-1]
    w = w.reshape(lead + (N_HEADS, d_a + d_b))
    a = w[..., :d_a].reshape(lead + (N_HEADS * d_a,))
    b = w[..., d_a:].reshape(lead + (N_HEADS * d_b,))
    return jnp.concatenate([a, b], axis=-1).astype(BF16)


def kernel(x_prompt, x_sample, cache_na_k, cache_na_v, cache_mla_ckv, cache_mla_krope, state_ret, c, c_ctx, w_mod, b_mod, norm_mix_pre, norm_mix_post, norm_ffn_pre, norm_ffn_post, w_in, hy_conv_w, hy_conv_b, hy_filt_w1, hy_filt_b1, hy_filt_w2, hy_filt_b2, hy_filt_w3, hy_bias, na_rpb, mla_q_norm, mla_w_uq, mla_kv_norm, mla_w_ukv, ret_decay, ret_gn, w_out, router_w, router_b, moe_w1, moe_b1, moe_w2, moe_b2):
    nb_c, l_c, d = x_prompt.shape
    nb_l, l_l, _ = x_sample.shape
    depth = w_in.shape[0]
    past = cache_na_k.shape[3]
    m_c = nb_c * l_c
    m_l = nb_l * l_l
    n_exp = router_w.shape[2]

    x = jnp.concatenate([x_prompt.reshape(m_c, d), x_sample.reshape(m_l, d)], axis=0)
    cv = jnp.zeros((8, d), F32).at[0].set(c_ctx).at[1:1 + nb_l].set(c)

    def grp_fn(tm):
        nc = m_c // tm
        per = l_l // tm
        return lambda i: jnp.where(i < nc, 0, 1 + (i - nc) // per)

    w_in_p = _prep_w_in(w_in)
    w_out_b = w_out.astype(BF16)
    w_uq_p = _prep_heads(mla_w_uq, MLA_NOPE, MLA_ROPE)
    w_ukv_p = _prep_heads(mla_w_ukv, MLA_NOPE, HEAD_DIM)
    rw_p = jnp.pad(router_w, ((0, 0), (0, 0), (0, LANES - n_exp)))
    rw_hi = rw_p.astype(BF16)
    rw_lo = (rw_p - rw_hi.astype(F32)).astype(BF16)
    rb_p = jnp.pad(router_b, ((0, 0), (0, LANES - n_exp)), constant_values=-jnp.inf)
    tables_c = _dft_tables(l_c)
    tables_l = _dft_tables(l_l)
    lg_all = jax.nn.log_sigmoid(ret_decay.astype(F32))
    ckv_cache_rows = cache_mla_ckv.reshape(nb_l * depth * past, MLA_KV_LORA)

    new_k, new_v, new_ckv, new_kr, new_s = [], [], [], [], []
    for layer in range(depth):
        lw = {"hy_conv_w": hy_conv_w[layer], "hy_conv_b": hy_conv_b[layer],
              "hy_filt_w1": hy_filt_w1[layer], "hy_filt_b1": hy_filt_b1[layer],
              "hy_filt_w2": hy_filt_w2[layer], "hy_filt_b2": hy_filt_b2[layer],
              "hy_filt_w3": hy_filt_w3[layer], "hy_bias": hy_bias[layer]}
        mod = _modulation(cv, w_mod, b_mod, layer)
        sa, ca, ga, sf, cf, gf = [a.reshape(8, 1, d) for a in jnp.split(mod, 6, axis=-1)]

        proj = _norm_matmul(x, 0, d, w_in_p[layer], tm=512, tn=512, gain=norm_mix_pre[layer],
                            mod=(ca, sa), grp=grp_fn(512), name="proj_in")

        y_hy_c = _hyena(proj, 0, nb_c, l_c, lw, tables_c)
        y_hy_l = _hyena(proj, m_c, nb_l, l_l, lw, tables_l)

        y_na_c, k_c, v_c = _na_context(proj, nb_c, l_c)
        y_na_l = _na_latent(proj, m_c, nb_l, l_l, cache_na_k, cache_na_v, layer, _na_bias_table(na_rpb[layer]))
        new_k.append(k_c)
        new_v.append(v_c)

        q = _norm_matmul(proj, COL_QC // MLA_Q_LORA, MLA_Q_LORA, w_uq_p[layer], tm=512, tn=512,
                         gain=mla_q_norm[layer], name="mla_q")
        kv, ckv_n = _norm_matmul(proj, COL_CKV // MLA_KV_LORA, MLA_KV_LORA, w_ukv_p[layer], tm=512, tn=1024,
                                 gain=mla_kv_norm[layer], emit_h=True, out_dtype=BF16, name="mla_kv")
        kv_cache = jnp.concatenate(
            [_norm_matmul(ckv_cache_rows, 0, MLA_KV_LORA, w_ukv_p[layer], tm=past, tn=1024, out_dtype=BF16,
                          xrow0=b * depth + layer, m=past, name="mla_kv_cache") for b in range(nb_l)], axis=0)
        new_ckv.append(ckv_n[:m_c].reshape(nb_c, l_c, MLA_KV_LORA))
        kr_c = proj[:m_c, COL_KROPE:COL_KROPE + MLA_ROPE]
        new_kr.append(kr_c.reshape(nb_c, l_c, MLA_ROPE))
        y_mla_c = _mla_attention(
            (q, pl.BlockSpec((l_c, GROUP_W), lambda b, i: (b, 0))),
            (q, pl.BlockSpec((l_c, N_HEADS * MLA_ROPE), lambda b, i: (b, GROUP_W // (N_HEADS * MLA_ROPE)))),
            [((kv, pl.BlockSpec((l_c, GROUP_W), lambda b, i: (b, 0))),
              (proj, pl.BlockSpec((l_c, LANES), lambda b, i: (b, COL_KROPE // LANES))),
              (kv, pl.BlockSpec((l_c, GROUP_W), lambda b, i: (b, 1))))],
            nb=nb_c, lq=l_c, tq=l_c, name="mla_context")
        q_l = q[m_c:]
        qr_l = _axial_rope(q_l[:, GROUP_W:].reshape(nb_l, l_l, N_HEADS, MLA_ROPE).transpose(0, 2, 1, 3))
        qr_l = qr_l.transpose(0, 2, 1, 3).reshape(m_l, N_HEADS * MLA_ROPE)
        kr_l = _axial_rope(proj[m_c:, COL_KROPE:COL_KROPE + MLA_ROPE].reshape(nb_l, l_l, MLA_ROPE))
        kr_l = kr_l.reshape(m_l, MLA_ROPE)
        tq = 256
        nq = l_l // tq
        rb_l = m_c // l_l
        y_mla_l = _mla_attention(
            (q, pl.BlockSpec((tq, GROUP_W), lambda b, i: (m_c // tq + b * nq + i, 0))),
            (qr_l, pl.BlockSpec((tq, N_HEADS * MLA_ROPE), lambda b, i: (b * nq + i, 0))),
            [((kv_cache, pl.BlockSpec((past, GROUP_W), lambda b, i: (b, 0))),
              (cache_mla_krope, pl.BlockSpec((None, None, past, MLA_ROPE), lambda b, i: (b, layer, 0, 0))),
              (kv_cache, pl.BlockSpec((past, GROUP_W), lambda b, i: (b, 1)))),
             ((kv, pl.BlockSpec((l_l, GROUP_W), lambda b, i: (rb_l + b, 0))),
              (kr_l, pl.BlockSpec((l_l, MLA_ROPE), lambda b, i: (b, 0))),
              (kv, pl.BlockSpec((l_l, GROUP_W), lambda b, i: (rb_l + b, 1))))],
            nb=nb_l, lq=l_l, tq=tq, name="mla_latent")

        lg = lg_all[layer]
        y_ret_c, s_c = _retention(proj, 0, nb_c, l_c, l_c, lg, ret_gn[layer], emit_state=True)
        y_ret_l = _retention(proj, m_c, nb_l, l_l, 256, lg, ret_gn[layer], state=state_ret, layer=layer)
        new_s.append(s_c)

        yp = _out_matmul((y_hy_c, y_na_c, y_mla_c, y_ret_c), (y_hy_l, y_na_l, y_mla_l, y_ret_l), w_out_b[layer])
        x1, h_ffn, top_i, top_g, pos, cnt = _post_router(
            yp, x, norm_mix_post[layer], ga, norm_ffn_pre[layer], cf, sf, rw_hi[layer], rw_lo[layer],
            rb_p[layer:layer + 1], grp_fn(256))

        counts = cnt[0, :n_exp].astype(jnp.int32)
        y_rows, dest = _moe_experts(h_ffn, top_i[:, :TOP_K], pos[:, :TOP_K], counts,
                                    moe_w1, moe_b1, moe_w2, moe_b2, layer)
        x = _gather_combine(y_rows, dest, top_g, x1, norm_ffn_post[layer], gf, grp_fn(128))

    return (x[:m_c].reshape(nb_c, l_c, d), x[m_c:].reshape(nb_l, l_l, d),
            jnp.stack(new_k, axis=1), jnp.stack(new_v, axis=1), jnp.stack(new_ckv, axis=1),
            jnp.stack(new_kr, axis=1), jnp.stack(new_s, axis=1))
```

```python
import functools
import math

import jax
import jax.numpy as jnp
from jax import lax
from jax.experimental import pallas as pl
from jax.experimental.pallas import tpu as pltpu

F32 = jnp.float32
BF16 = jnp.bfloat16
SDS = jax.ShapeDtypeStruct

EPS = 1e-6
GRID_W = 64
HEAD_DIM = 128
N_HEADS = 8
GROUP_W = N_HEADS * HEAD_DIM
HY_BANDS = 16
HY_TARGET = 1e-2
HY_FAST_DECAY = 0.3
HY_SLOW_DECAY = 1.5
NA_WR = 8
NA_WC = 16
MLA_Q_LORA = 768
MLA_KV_LORA = 512
MLA_ROPE = 64
MLA_NOPE = 128
MLA_SCALE = (MLA_NOPE + MLA_ROPE) ** -0.5
ROPE_BASE = 10000.0
TOP_K = 4
SWIGLU_LIMIT = 7.0
SWIGLU_ALPHA = 1.702
LANES = 128

COL_HY = 0
COL_NA = 3072
COL_RET = 6144
COL_CKV = 10240
COL_QC = 10752
COL_KROPE = 11520
N_PROJ = 11776

VMEM_LIMIT_MB = 56


def _cp(sem, vmem_mb=48):
    return pltpu.CompilerParams(dimension_semantics=sem, vmem_limit_bytes=min(vmem_mb, VMEM_LIMIT_MB) * 2**20)


def _dot(a, b):
    return jnp.dot(a, b, preferred_element_type=F32)


def _dot_nt(a, b):
    return lax.dot_general(a, b, (((1,), (1,)), ((), ())), preferred_element_type=F32)


def _dot_tn(a, b):
    return lax.dot_general(a, b, (((0,), (0,)), ((), ())), preferred_element_type=F32)


def _silu(x):
    return x * jax.nn.sigmoid(x)


def _mod_body(c_ref, w_ref, b_ref, o_ref):
    s = _silu(c_ref[...])
    o_ref[...] = _dot(s.astype(BF16), w_ref[...].astype(BF16)) + b_ref[...]


def _modulation(cv, w_mod, b_mod, layer):
    _, d, n = w_mod.shape
    tn = 512
    return pl.pallas_call(
        _mod_body,
        grid=(n // tn,),
        in_specs=[pl.BlockSpec((8, d), lambda j: (0, 0)),
                  pl.BlockSpec((None, d, tn), lambda j: (layer, 0, j)),
                  pl.BlockSpec((None, 1, tn), lambda j: (layer, 0, j))],
        out_specs=pl.BlockSpec((8, tn), lambda j: (0, j)),
        out_shape=SDS((8, n), F32),
        compiler_params=_cp(("arbitrary",), 40),
        name="modulation",
    )(cv, w_mod, b_mod.reshape(b_mod.shape[0], 1, n))


def _norm_matmul_body(*refs, do_norm, has_mod, emit_h):
    it = iter(refs)
    x_ref = next(it)
    g_ref = next(it) if do_norm else None
    sc_ref = next(it) if has_mod else None
    sh_ref = next(it) if has_mod else None
    w_ref = next(it)
    o_ref = next(it)
    h_ref = next(it) if emit_h else None
    hs_ref = next(it)

    @pl.when(pl.program_id(1) == 0)
    def _():
        h = x_ref[...].astype(F32)
        if do_norm:
            h = h * lax.rsqrt(jnp.mean(h * h, axis=-1, keepdims=True) + EPS) * g_ref[...]
        if has_mod:
            h = h * (1.0 + sc_ref[...]) + sh_ref[...]
        if emit_h:
            h_ref[...] = h
        hs_ref[...] = h.astype(BF16)

    o_ref[...] = _dot(hs_ref[...], w_ref[...].astype(BF16)).astype(o_ref.dtype)


def _norm_matmul(x, xcol, k, w, *, tm, tn, gain=None, mod=None, grp=None, emit_h=False,
                 out_dtype=F32, xrow0=0, m=None, name="norm_matmul"):
    m = x.shape[0] if m is None else m
    n = w.shape[1]
    do_norm = gain is not None
    has_mod = mod is not None
    in_specs = [pl.BlockSpec((tm, k), lambda i, j: (i + xrow0, xcol))]
    args = [x]
    if do_norm:
        in_specs.append(pl.BlockSpec((1, k), lambda i, j: (0, 0)))
        args.append(gain.reshape(1, k))
    if has_mod:
        for a in mod:
            in_specs.append(pl.BlockSpec((None, 1, k), lambda i, j: (grp(i), 0, 0)))
            args.append(a)
    in_specs.append(pl.BlockSpec((k, tn), lambda i, j: (0, j)))
    args.append(w)
    out_specs = [pl.BlockSpec((tm, tn), lambda i, j: (i, j))]
    out_shape = [SDS((m, n), out_dtype)]
    if emit_h:
        out_specs.append(pl.BlockSpec((tm, k), lambda i, j: (i, 0)))
        out_shape.append(SDS((m, k), F32))
    res = pl.pallas_call(
        functools.partial(_norm_matmul_body, do_norm=do_norm, has_mod=has_mod, emit_h=emit_h),
        grid=(m // tm, n // tn),
        in_specs=in_specs, out_specs=out_specs, out_shape=out_shape,
        scratch_shapes=[pltpu.VMEM((tm, k), BF16)],
        compiler_params=_cp(("arbitrary", "arbitrary"), 48),
        name=name,
    )(*args)
    return res if emit_h else res[0]


def _dft_tables(l):
    n = 2 * l
    f = jnp.arange(l, dtype=jnp.int32)
    ang = ((f[:, None] * f[None, :]) % n).astype(F32) * (2.0 * math.pi / n)
    fc = jnp.cos(ang)
    alt = jnp.where(f % 2 == 0, 1.0, -1.0).astype(F32)
    fs = jnp.where(f[:, None] == 0, alt[None, :], jnp.sin(ang))
    return fc.astype(BF16), fs.astype(BF16), fs.T.astype(BF16)


def _hy_spec_body(fc_ref, fs_ref, hf_ref, hb_ref, a_ref, b_ref, *, l):
    hf = hf_ref[...]
    row = lax.broadcasted_iota(jnp.int32, hf.shape, 0)
    hb = jnp.where(row == 0, 0.0, hb_ref[...])
    a = _dot(fc_ref[...], (hf + hb).astype(BF16))
    p1 = _dot(fs_ref[...], hf.astype(BF16))
    p2 = _dot(fs_ref[...], hb.astype(BF16))
    orow = lax.broadcasted_iota(jnp.int32, a.shape, 0)
    first = jnp.logical_and(pl.program_id(0) == 0, orow == 0)
    wgt = jnp.where(first, 0.5 / l, 1.0 / l)
    a_ref[...] = wgt * a
    b_ref[...] = wgt * jnp.where(first, p1 + p2, p1 - p2)


def _hy_spectra(fc, fs, hf, hb):
    l, c2 = hf.shape
    tf = min(l, 256)
    tc = 512
    return pl.pallas_call(
        functools.partial(_hy_spec_body, l=l),
        grid=(l // tf, c2 // tc),
        in_specs=[pl.BlockSpec((tf, l), lambda i, j: (i, 0)),
                  pl.BlockSpec((tf, l), lambda i, j: (i, 0)),
                  pl.BlockSpec((l, tc), lambda i, j: (0, j)),
                  pl.BlockSpec((l, tc), lambda i, j: (0, j))],
        out_specs=[pl.BlockSpec((tf, tc), lambda i, j: (i, j))] * 2,
        out_shape=[SDS((l, c2), F32)] * 2,
        compiler_params=_cp(("arbitrary", "arbitrary"), 40),
        name="hyena_filter_spectra",
    )(fc, fs, hf, hb)


def _short_conv(x, w, b):
    l = x.shape[0]
    row = lax.broadcasted_iota(jnp.int32, x.shape, 0)
    xm = jnp.where(row == 0, 0.0, pltpu.roll(x, 1, 0))
    xp = jnp.where(row == l - 1, 0.0, pltpu.roll(x, l - 1, 0))
    return xm * w[0:1, :] + x * w[1:2, :] + xp * w[2:3, :] + b


def _hy_conv_body(*refs, first_order, nf):
    it = iter(refs)
    z_ref = next(it)
    zw_ref = next(it) if first_order else None
    zb_ref = next(it) if first_order else None
    gt_ref, gw_ref, gb_ref, skip_ref = next(it), next(it), next(it), next(it)
    ah_ref, bh_ref, fc_ref, fs_ref, gc_ref, gs_ref = (next(it) for _ in range(6))
    o_ref = next(it)
    zf_scr, zb_scr, acc = next(it), next(it), next(it)
    fb = pl.program_id(2)

    @pl.when(fb == 0)
    def _():
        z = z_ref[...].astype(F32)
        if first_order:
            z = _short_conv(z, zw_ref[...], zb_ref[...])
        zf_scr[...] = z
        zb_scr[...] = z.astype(BF16)
        acc[...] = jnp.zeros_like(acc)

    zb = zb_scr[...]
    uc = _dot(fc_ref[...], zb)
    us = _dot(fs_ref[...], zb)
    ah = ah_ref[...]
    bh = bh_ref[...]
    row = lax.broadcasted_iota(jnp.int32, uc.shape, 0)
    first = jnp.logical_and(fb == 0, row == 0)
    usb = us * bh
    yc = uc * ah - jnp.where(first, 0.0, usb)
    ys = jnp.where(first, usb, uc * bh + us * ah)
    acc[...] += _dot(gc_ref[...], yc.astype(BF16)) + _dot(gs_ref[...], ys.astype(BF16))

    @pl.when(fb == nf - 1)
    def _():
        gate = _short_conv(gt_ref[...].astype(F32), gw_ref[...], gb_ref[...])
        o_ref[...] = (gate * (acc[...] + zf_scr[...] * skip_ref[...])).astype(o_ref.dtype)


def _hy_conv(z, gate_col, proj, rb, conv_w, conv_b, skip, ah, bh, spec_col0, tables, *,
             nb, l, z_col, out_dtype):
    fc, fs, gs = tables
    first_order = z_col is not None
    tc = 512 if l > 512 else GROUP_W
    tf = min(l, 256)
    nf = l // tf

    def proj_specs(col):
        return [pl.BlockSpec((l, tc), lambda b, c, f: (b + rb, col // tc + c)),
                pl.BlockSpec((3, tc), lambda b, c, f: (0, (col - COL_HY) // tc + c)),
                pl.BlockSpec((1, tc), lambda b, c, f: (0, (col - COL_HY) // tc + c))]

    if first_order:
        in_specs = proj_specs(z_col)
        args = [proj, conv_w, conv_b]
    else:
        in_specs = [pl.BlockSpec((l, tc), lambda b, c, f: (b, c))]
        args = [z]
    in_specs += proj_specs(gate_col)
    args += [proj, conv_w, conv_b]
    in_specs += [pl.BlockSpec((1, tc), lambda b, c, f: (0, c)),
                 pl.BlockSpec((tf, tc), lambda b, c, f: (f, spec_col0 // tc + c)),
                 pl.BlockSpec((tf, tc), lambda b, c, f: (f, spec_col0 // tc + c)),
                 pl.BlockSpec((tf, l), lambda b, c, f: (f, 0)),
                 pl.BlockSpec((tf, l), lambda b, c, f: (f, 0)),
                 pl.BlockSpec((l, tf), lambda b, c, f: (0, f)),
                 pl.BlockSpec((l, tf), lambda b, c, f: (0, f))]
    args += [skip, ah, bh, fc, fs, fc, gs]
    return pl.pallas_call(
        functools.partial(_hy_conv_body, first_order=first_order, nf=nf),
        grid=(nb, GROUP_W // tc, nf),
        in_specs=in_specs,
        out_specs=pl.BlockSpec((l, tc), lambda b, c, f: (b, c)),
        out_shape=SDS((nb * l, GROUP_W), out_dtype),
        scratch_shapes=[pltpu.VMEM((l, tc), F32), pltpu.VMEM((l, tc), BF16), pltpu.VMEM((l, tc), F32)],
        compiler_params=_cp(("arbitrary", "arbitrary", "arbitrary"), 52),
        name="hyena_conv",
    )(*args)


def _hyena_filters(l, w1, b1, w2, b2, w3):
    hp = lax.Precision.HIGHEST
    pos = jnp.arange(l, dtype=F32)
    t = pos / l
    bands = jnp.linspace(1e-4, HY_BANDS - 1, HY_BANDS, dtype=F32)
    ang = (2.0 * math.pi * t)[:, None] * bands[None, :]
    z = jnp.concatenate([t[:, None], jnp.cos(ang), -jnp.sin(ang)], axis=-1)
    h = jnp.sin(jnp.dot(z, w1, precision=hp) + b1)
    h = jnp.sin(jnp.dot(h, w2, precision=hp) + b2)
    h = jnp.dot(h, w3, precision=hp)
    deltas = jnp.abs(jnp.linspace(math.log(HY_TARGET) / HY_FAST_DECAY,
                                  math.log(HY_TARGET) / HY_SLOW_DECAY, GROUP_W, dtype=F32))
    window = jnp.exp(-t[:, None] * deltas[None, :])
    filt = h.reshape(l, 2, 2, GROUP_W) * window[:, None, None, :]
    return filt[:, :, 0, :].reshape(l, 2 * GROUP_W), filt[:, :, 1, :].reshape(l, 2 * GROUP_W)


def _hyena(proj, row0, nb, l, lw, tables):
    fc, fs, gs = tables
    hf, hb = _hyena_filters(l, lw["hy_filt_w1"], lw["hy_filt_b1"], lw["hy_filt_w2"], lw["hy_filt_b2"],
                            lw["hy_filt_w3"])
    ah, bh = _hy_spectra(fc, fs, hf, hb)
    cw = lw["hy_conv_w"]
    cb = lw["hy_conv_b"].reshape(1, -1)
    skip = lw["hy_bias"]
    rb = row0 // l
    z1 = _hy_conv(None, COL_HY, proj, rb, cw, cb, skip[0:1], ah, bh, 0, tables,
                  nb=nb, l=l, z_col=COL_HY + 2 * GROUP_W, out_dtype=F32)
    return _hy_conv(z1, COL_HY + GROUP_W, proj, rb, cw, cb, skip[1:2], ah, bh, GROUP_W, tables,
                    nb=nb, l=l, z_col=None, out_dtype=BF16)


def _softmax_attend(scores, values):
    m = functools.reduce(jnp.maximum, [jnp.max(s, axis=-1, keepdims=True) for s in scores])
    ps = [jnp.exp(s - m) for s in scores]
    den = functools.reduce(jnp.add, [jnp.sum(p, axis=-1, keepdims=True) for p in ps])
    o = functools.reduce(jnp.add, [_dot(p.astype(BF16), v) for p, v in zip(ps, values)])
    return o / den


def _na_ctx_body(q_ref, k_ref, v_ref, o_ref, kc_ref, vc_ref):
    scale = HEAD_DIM ** -0.5
    for h in range(N_HEADS):
        hs = slice(h * HEAD_DIM, (h + 1) * HEAD_DIM)
        k = k_ref[:, hs]
        v = v_ref[:, hs]
        kc_ref[h] = k
        vc_ref[h] = v
        s = _dot_nt(q_ref[:, hs].astype(BF16), k.astype(BF16)) * scale
        o_ref[:, hs] = _softmax_attend([s], [v.astype(BF16)]).astype(o_ref.dtype)


def _na_context(proj, nb, l):
    c0 = COL_NA // GROUP_W
    return pl.pallas_call(
        _na_ctx_body,
        grid=(nb,),
        in_specs=[pl.BlockSpec((l, GROUP_W), lambda b: (b, c0)),
                  pl.BlockSpec((l, GROUP_W), lambda b: (b, c0 + 1)),
                  pl.BlockSpec((l, GROUP_W), lambda b: (b, c0 + 2))],
        out_specs=[pl.BlockSpec((l, GROUP_W), lambda b: (b, 0)),
                   pl.BlockSpec((None, N_HEADS, l, HEAD_DIM), lambda b: (b, 0, 0, 0)),
                   pl.BlockSpec((None, N_HEADS, l, HEAD_DIM), lambda b: (b, 0, 0, 0))],
        out_shape=[SDS((nb * l, GROUP_W), BF16),
                   SDS((nb, N_HEADS, l, HEAD_DIM), F32),
                   SDS((nb, N_HEADS, l, HEAD_DIM), F32)],
        compiler_params=_cp(("arbitrary",), 32),
        name="na_context",
    )(proj, proj, proj)


def _na_bias_table(rpb):
    cidx = jnp.arange(GRID_W)
    col_start = jnp.clip(cidx - NA_WC // 2, 0, GRID_W - NA_WC)
    col_in = (cidx[None, :] >= col_start[:, None]) & (cidx[None, :] < col_start[:, None] + NA_WC)
    col_off = jnp.clip(cidx[None, :] - cidx[:, None], -(NA_WC - 1), NA_WC - 1) + (NA_WC - 1)
    t = rpb[:, :, col_off].astype(F32)
    t = jnp.where(col_in[None, None], t, -jnp.inf)
    return t.transpose(0, 2, 1, 3).reshape(rpb.shape[0], GRID_W, -1)


def _na_lat_body(q_ref, k_ref, v_ref, kc_ref, vc_ref, bt_ref, o_ref, *, rows):
    scale = HEAD_DIM ** -0.5
    wr = min(NA_WR, rows)
    kcx = kc_ref[...].astype(BF16)
    vcx = vc_ref[...].astype(BF16)
    nloc = wr * GRID_W
    bt = bt_ref[...]
    for r in range(rows):
        rs = min(max(r - wr // 2, 0), rows - wr)
        off = rs - r + (NA_WR - 1)
        q = q_ref[r * GRID_W:(r + 1) * GRID_W, :].astype(BF16)
        kl = k_ref[rs * GRID_W:rs * GRID_W + nloc, :].astype(BF16)
        vl = v_ref[rs * GRID_W:rs * GRID_W + nloc, :].astype(BF16)
        s_loc = _dot_nt(q, kl) * scale + bt[:, off * GRID_W:off * GRID_W + nloc]
        s_ctx = _dot_nt(q, kcx) * scale
        o_ref[r * GRID_W:(r + 1) * GRID_W, :] = _softmax_attend([s_loc, s_ctx], [vl, vcx]).astype(o_ref.dtype)


def _na_latent(proj, row0, nb, l, cache_k, cache_v, layer, bias_table):
    rows = l // GRID_W
    rb = row0 // l
    c0 = COL_NA // HEAD_DIM
    past = cache_k.shape[3]
    cspec = pl.BlockSpec((None, None, None, past, HEAD_DIM), lambda b, h: (b, layer, h, 0, 0))
    return pl.pallas_call(
        functools.partial(_na_lat_body, rows=rows),
        grid=(nb, N_HEADS),
        in_specs=[pl.BlockSpec((l, HEAD_DIM), lambda b, h: (b + rb, c0 + h)),
                  pl.BlockSpec((l, HEAD_DIM), lambda b, h: (b + rb, c0 + N_HEADS + h)),
                  pl.BlockSpec((l, HEAD_DIM), lambda b, h: (b + rb, c0 + 2 * N_HEADS + h)),
                  cspec, cspec,
                  pl.BlockSpec((None, GRID_W, bias_table.shape[2]), lambda b, h: (h, 0, 0))],
        out_specs=pl.BlockSpec((l, HEAD_DIM), lambda b, h: (b, h)),
        out_shape=SDS((nb * l, GROUP_W), BF16),
        compiler_params=_cp(("arbitrary", "arbitrary"), 32),
        name="na_latent",
    )(proj, proj, proj, cache_k, cache_v, bias_table)


def _mla_attn_body(*refs, nseg):
    qn_ref, qr_ref = refs[0], refs[1]
    segs = [refs[2 + 3 * i: 5 + 3 * i] for i in range(nseg)]
    o_ref = refs[2 + 3 * nseg]
    krs = [kr_ref[...][:, :MLA_ROPE].astype(BF16) for _, kr_ref, _ in segs]
    qr_all = qr_ref[...].astype(BF16)
    for h in range(N_HEADS):
        hs = slice(h * HEAD_DIM, (h + 1) * HEAD_DIM)
        qn = qn_ref[:, hs].astype(BF16)
        qr = qr_all[:, h * MLA_ROPE:(h + 1) * MLA_ROPE]
        scores = [(_dot_nt(qn, kn_ref[:, hs].astype(BF16)) + _dot_nt(qr, kr)) * MLA_SCALE
                  for (kn_ref, _, _), kr in zip(segs, krs)]
        vals = [v_ref[:, hs].astype(BF16) for _, _, v_ref in segs]
        o_ref[:, hs] = _softmax_attend(scores, vals).astype(o_ref.dtype)


def _mla_attention(qn, qr, segs, *, nb, lq, tq, name):
    arrays = [qn[0], qr[0]]
    specs = [qn[1], qr[1]]
    for s in segs:
        for a, sp in s:
            arrays.append(a)
            specs.append(sp)
    return pl.pallas_call(
        functools.partial(_mla_attn_body, nseg=len(segs)),
        grid=(nb, lq // tq),
        in_specs=specs,
        out_specs=pl.BlockSpec((tq, GROUP_W), lambda b, i: (b * (lq // tq) + i, 0)),
        out_shape=SDS((nb * lq, GROUP_W), BF16),
        compiler_params=_cp(("arbitrary", "arbitrary"), 48),
        name=name,
    )(*arrays)


def _axial_rope(x):
    l = x.shape[-2]
    pos = jnp.arange(l)
    half = MLA_ROPE // 2
    freqs = jnp.power(ROPE_BASE, -jnp.arange(0, half, 2, dtype=F32) / half)

    def rot(xa, p):
        ang = p.astype(F32)[:, None] * freqs[None, :]
        cos, sin = jnp.cos(ang), jnp.sin(ang)
        a1, a2 = jnp.split(xa, 2, axis=-1)
        return jnp.concatenate([a1 * cos - a2 * sin, a2 * cos + a1 * sin], axis=-1)

    return jnp.concatenate([rot(x[..., :half], pos // GRID_W), rot(x[..., half:], pos % GRID_W)], axis=-1)


def _ret_body(*refs, l, tq, has_state, emit_state):
    it = iter(refs)
    lg_ref = next(it)
    q_ref, k_ref, v_ref, g_ref, gn_ref = (next(it) for _ in range(5))
    s0_ref = next(it) if has_state else None
    o_ref = next(it)
    so_ref = next(it) if emit_state else None
    t0 = pl.program_id(1) * tq
    ti = lax.broadcasted_iota(jnp.int32, (tq, l), 0) + t0
    si = lax.broadcasted_iota(jnp.int32, (tq, l), 1)
    d = (ti - si).astype(F32)
    tcol = (lax.broadcasted_iota(jnp.int32, (tq, 1), 0) + t0).astype(F32)
    scol = lax.broadcasted_iota(jnp.int32, (l, 1), 0).astype(F32)
    kscale = HEAD_DIM ** -0.5
    for h in range(N_HEADS):
        hs = slice(h * HEAD_DIM, (h + 1) * HEAD_DIM)
        lgf = lg_ref[0, h]
        lgb = lg_ref[1, h]
        q = q_ref[:, hs]
        k = k_ref[:, hs] * kscale
        v = v_ref[:, hs].astype(BF16)
        s = _dot_nt(q.astype(BF16), k.astype(BF16))
        w = jnp.exp(jnp.where(d > 0, lgf, -lgb) * d)
        w = jnp.where(d == 0, 2.0, w)
        o = _dot((s * w).astype(BF16), v)
        if has_state:
            qf = q * jnp.exp(lgf * (tcol + 1.0))
            qb = q * jnp.exp(lgb * (l - tcol))
            o = o + _dot(qf.astype(BF16), s0_ref[0, h].astype(BF16)) + _dot(qb.astype(BF16), s0_ref[1, h].astype(BF16))
        o = o * lax.rsqrt(jnp.mean(o * o, axis=-1, keepdims=True) + EPS) * gn_ref[:, hs]
        o_ref[:, hs] = (o * _silu(g_ref[:, hs])).astype(o_ref.dtype)
        if emit_state:
            kf = k * jnp.exp(lgf * (l - 1.0 - scol))
            kb = k * jnp.exp(lgb * scol)
            so_ref[0, h] = _dot_tn(kf.astype(BF16), v)
            so_ref[1, h] = _dot_tn(kb.astype(BF16), v)


def _retention(proj, row0, nb, l, tq, lg, gn, state=None, layer=0, emit_state=False):
    assert not emit_state or tq == l
    nq = l // tq
    rbq = row0 // tq
    rbl = row0 // l
    c0 = COL_RET // GROUP_W
    in_specs = [pl.BlockSpec(memory_space=pltpu.SMEM),
                pl.BlockSpec((tq, GROUP_W), lambda b, i: (rbq + b * nq + i, c0)),
                pl.BlockSpec((l, GROUP_W), lambda b, i: (rbl + b, c0 + 1)),
                pl.BlockSpec((l, GROUP_W), lambda b, i: (rbl + b, c0 + 2)),
                pl.BlockSpec((tq, GROUP_W), lambda b, i: (rbq + b * nq + i, c0 + 3)),
                pl.BlockSpec((1, GROUP_W), lambda b, i: (0, 0))]
    args = [lg, proj, proj, proj, proj, gn.reshape(1, GROUP_W)]
    if state is not None:
        in_specs.append(pl.BlockSpec((None, None, 2, N_HEADS, HEAD_DIM, HEAD_DIM),
                                     lambda b, i: (b, layer, 0, 0, 0, 0)))
        args.append(state)
    out_specs = [pl.BlockSpec((tq, GROUP_W), lambda b, i: (b * nq + i, 0))]
    out_shape = [SDS((nb * l, GROUP_W), BF16)]
    if emit_state:
        out_specs.append(pl.BlockSpec((None, 2, N_HEADS, HEAD_DIM, HEAD_DIM), lambda b, i: (b, 0, 0, 0, 0)))
        out_shape.append(SDS((nb, 2, N_HEADS, HEAD_DIM, HEAD_DIM), F32))
    res = pl.pallas_call(
        functools.partial(_ret_body, l=l, tq=tq, has_state=state is not None, emit_state=emit_state),
        grid=(nb, nq),
        in_specs=in_specs, out_specs=out_specs, out_shape=out_shape,
        compiler_params=_cp(("arbitrary", "arbitrary"), 48),
        name="retention",
    )(*args)
    return res if emit_state else res[0]


def _out_matmul_body(*refs, nc):
    ctx, lat, w_ref, o_ref = refs[0:4], refs[4:8], refs[8], refs[9]

    def project(ys):
        acc = _dot(ys[0][...], w_ref[0 * GROUP_W:1 * GROUP_W, :])
        acc += _dot(ys[1][...], w_ref[1 * GROUP_W:2 * GROUP_W, :])
        acc += _dot(ys[2][...], w_ref[2 * GROUP_W:3 * GROUP_W, :])
        acc += _dot(ys[3][...], w_ref[3 * GROUP_W:4 * GROUP_W, :])
        o_ref[...] = acc

    @pl.when(pl.program_id(1) < nc)
    def _():
        project(ctx)

    @pl.when(pl.program_id(1) >= nc)
    def _():
        project(lat)


def _out_matmul(ys_ctx, ys_lat, w_out, *, tm=512, tn=1024):
    m_c = ys_ctx[0].shape[0]
    m = m_c + ys_lat[0].shape[0]
    nc = m_c // tm
    d = w_out.shape[1]
    cspec = pl.BlockSpec((tm, GROUP_W), lambda j, i: (jnp.minimum(i, nc - 1), 0))
    lspec = pl.BlockSpec((tm, GROUP_W), lambda j, i: (jnp.maximum(i - nc, 0), 0))
    return pl.pallas_call(
        functools.partial(_out_matmul_body, nc=nc),
        grid=(d // tn, m // tm),
        in_specs=[cspec] * 4 + [lspec] * 4 + [pl.BlockSpec((4 * GROUP_W, tn), lambda j, i: (0, j))],
        out_specs=pl.BlockSpec((tm, tn), lambda j, i: (i, j)),
        out_shape=SDS((m, d), F32),
        compiler_params=_cp(("arbitrary", "arbitrary"), 48),
        name="out_proj",
    )(*ys_ctx, *ys_lat, w_out)


def _topk(logits):
    lane = lax.broadcasted_iota(jnp.int32, logits.shape, 1)
    work = logits
    idx_out = jnp.zeros(logits.shape, jnp.int32)
    val_out = jnp.full(logits.shape, -jnp.inf, F32)
    chosen = jnp.zeros(logits.shape, F32)
    idxs = []
    for k in range(TOP_K):
        m = jnp.max(work, axis=-1, keepdims=True)
        idx = jnp.min(jnp.where(work == m, lane, LANES), axis=-1, keepdims=True)
        idx_out = jnp.where(lane == k, idx, idx_out)
        val_out = jnp.where(lane == k, m, val_out)
        chosen = jnp.where(lane == idx, 1.0, chosen)
        work = jnp.where(lane == idx, -jnp.inf, work)
        idxs.append(idx)
    e = jnp.exp(val_out - jnp.max(val_out, axis=-1, keepdims=True))
    return idx_out, e / jnp.sum(e, axis=-1, keepdims=True), chosen, idxs


def _post_router_body(yp_ref, x_ref, gpost_ref, ga_ref, gpre_ref, cf_ref, sf_ref, rwh_ref, rwl_ref, rb_ref,
                      x1_ref, h_ref, ti_ref, tg_ref, pos_ref, cnt_ref, carry):
    @pl.when(pl.program_id(0) == 0)
    def _():
        carry[...] = jnp.zeros_like(carry)

    y = yp_ref[...]
    y = y * lax.rsqrt(jnp.mean(y * y, axis=-1, keepdims=True) + EPS) * gpost_ref[...]
    x1 = x_ref[...] + ga_ref[...] * y
    x1_ref[...] = x1
    h = x1 * lax.rsqrt(jnp.mean(x1 * x1, axis=-1, keepdims=True) + EPS) * gpre_ref[...]
    h = h * (1.0 + cf_ref[...]) + sf_ref[...]
    h_ref[...] = h.astype(h_ref.dtype)
    h_hi = h.astype(BF16)
    h_lo = (h - h_hi.astype(F32)).astype(BF16)
    logits = _dot(h_hi, rwh_ref[...]) + _dot(h_lo, rwh_ref[...]) + _dot(h_hi, rwl_ref[...]) + rb_ref[...]
    idx_out, gates, chosen, idxs = _topk(logits)
    ti_ref[...] = idx_out
    tg_ref[...] = gates
    tm = chosen.shape[0]
    r = lax.broadcasted_iota(jnp.int32, (tm, tm), 0)
    c = lax.broadcasted_iota(jnp.int32, (tm, tm), 1)
    before = _dot(jnp.where(c < r, 1.0, 0.0).astype(BF16), chosen.astype(BF16)) + carry[0:1, :]
    lane = lax.broadcasted_iota(jnp.int32, chosen.shape, 1)
    pos = jnp.zeros(chosen.shape, jnp.int32)
    for k, idx in enumerate(idxs):
        pk = jnp.sum(jnp.where(lane == idx, before, 0.0), axis=-1, keepdims=True)
        pos = jnp.where(lane == k, pk.astype(jnp.int32), pos)
    pos_ref[...] = pos
    carry[...] = carry[...] + jnp.sum(chosen, axis=0, keepdims=True)
    cnt_ref[...] = carry[...]


def _post_router(yp, x, gpost, ga, gpre, cf, sf, rw_hi, rw_lo, rb, grp, *, tm=256):
    m, d = x.shape
    row = pl.BlockSpec((1, d), lambda i: (0, 0))
    modspec = pl.BlockSpec((None, 1, d), lambda i: (grp(i), 0, 0))
    full = pl.BlockSpec((tm, d), lambda i: (i, 0))
    small = pl.BlockSpec((tm, LANES), lambda i: (i, 0))
    rwspec = pl.BlockSpec((d, LANES), lambda i: (0, 0))
    return pl.pallas_call(
        _post_router_body,
        grid=(m // tm,),
        in_specs=[full, full, row, modspec, row, modspec, modspec, rwspec, rwspec,
                  pl.BlockSpec((1, LANES), lambda i: (0, 0))],
        out_specs=[full, full, small, small, small, pl.BlockSpec((8, LANES), lambda i: (0, 0))],
        out_shape=[SDS((m, d), F32), SDS((m, d), BF16), SDS((m, LANES), jnp.int32), SDS((m, LANES), F32),
                   SDS((m, LANES), jnp.int32), SDS((8, LANES), F32)],
        scratch_shapes=[pltpu.VMEM((8, LANES), F32)],
        compiler_params=_cp(("arbitrary",), 52),
        name="post_norm_router",
    )(yp, x, gpost.reshape(1, d), ga, gpre.reshape(1, d), cf, sf, rw_hi, rw_lo, rb)


def _deinterleave(hh):
    tm, tn = hh.shape
    lane = lax.broadcasted_iota(jnp.int32, (tm, LANES), 1)
    half = LANES // 2
    idx = jnp.where(lane < half, 2 * lane, 2 * (lane - half) + 1)
    ev, od = [], []
    for c in range(tn // (2 * LANES)):
        g0 = jnp.take_along_axis(hh[:, (2 * c) * LANES:(2 * c + 1) * LANES], idx, axis=1)
        g1 = jnp.take_along_axis(hh[:, (2 * c + 1) * LANES:(2 * c + 2) * LANES], idx, axis=1)
        ev.append(jnp.where(lane < half, g0, pltpu.roll(g1, half, 1)))
        od.append(jnp.where(lane < half, pltpu.roll(g0, half, 1), g1))
    return jnp.concatenate(ev, axis=1), jnp.concatenate(od, axis=1)


def _cast_weights(w_ref, wb_ref, chunk=256):
    k = w_ref.shape[0]

    def step(c, carry):
        r = pl.multiple_of(c * chunk, chunk)
        wb_ref[pl.ds(r, chunk), :] = w_ref[pl.ds(r, chunk), :].astype(BF16)
        return carry

    lax.fori_loop(0, k // chunk, step, 0)


ROW_BLOCK = 512
ROW_PAD = 256


def _swiglu(hh):
    glu, lin = _deinterleave(hh)
    glu = jnp.minimum(glu, SWIGLU_LIMIT)
    lin = jnp.clip(lin, -SWIGLU_LIMIT, SWIGLU_LIMIT)
    return glu * jax.nn.sigmoid(SWIGLU_ALPHA * glu) * (lin + 1.0)


def _expert_body(n5_ref, tl_ref, r0_ref, lf_ref, x_hbm, w_ref, b_ref, o_hbm, wb, xbuf, obuf, xtail, otail,
                 sem_in, sem_out, sem_tail, *, swiglu, tn_out):
    e = pl.program_id(0)
    n5 = n5_ref[e]
    tl = tl_ref[e]
    r0 = r0_ref[e]
    col0 = pl.multiple_of(pl.program_id(1) * tn_out, tn_out)

    def x_dma(b, slot):
        row = pl.multiple_of(r0 + b * ROW_BLOCK, ROW_PAD)
        return pltpu.make_async_copy(x_hbm.at[pl.ds(row, ROW_BLOCK), :], xbuf.at[slot], sem_in.at[slot])

    def o_dma(b, slot):
        row = pl.multiple_of(r0 + b * ROW_BLOCK, ROW_PAD)
        return pltpu.make_async_copy(obuf.at[slot], o_hbm.at[pl.ds(row, ROW_BLOCK), pl.ds(col0, tn_out)],
                                     sem_out.at[slot])

    def tail_in_dma(ee):
        row = pl.multiple_of(r0_ref[ee] + n5_ref[ee] * ROW_BLOCK, ROW_PAD)
        return pltpu.make_async_copy(x_hbm.at[pl.ds(row, ROW_PAD), :], xtail, sem_tail.at[0])

    def start_first_inputs(ee):
        @pl.when(n5_ref[ee] > 0)
        def _():
            row = pl.multiple_of(r0_ref[ee], ROW_PAD)
            pltpu.make_async_copy(x_hbm.at[pl.ds(row, ROW_BLOCK), :], xbuf.at[0], sem_in.at[0]).start()

        @pl.when(tl_ref[ee] == 1)
        def _():
            tail_in_dma(ee).start()

    trow = pl.multiple_of(r0 + n5 * ROW_BLOCK, ROW_PAD)
    tail_out = pltpu.make_async_copy(otail, o_hbm.at[pl.ds(trow, ROW_PAD), pl.ds(col0, tn_out)], sem_tail.at[1])

    def compute(x):
        hh = _dot(x, wb[...]) + b_ref[...]
        return _swiglu(hh) if swiglu else hh

    @pl.when(jnp.logical_and(e == 0, pl.program_id(1) == 0))
    def _():
        start_first_inputs(0)

    @pl.when(n5 + tl > 0)
    def _():
        _cast_weights(w_ref, wb)

        def loop(b, carry):
            slot = lax.rem(b, 2)
            x_dma(b, slot).wait()

            @pl.when(b + 1 < n5)
            def _():
                x_dma(b + 1, 1 - slot).start()

            @pl.when(b >= 2)
            def _():
                o_dma(b - 2, slot).wait()

            obuf[slot] = compute(xbuf[slot]).astype(obuf.dtype)
            o_dma(b, slot).start()
            return carry

        lax.fori_loop(0, n5, loop, 0)

        @pl.when(n5 >= 2)
        def _():
            o_dma(n5 - 2, lax.rem(n5, 2)).wait()

        @pl.when(n5 >= 1)
        def _():
            o_dma(n5 - 1, lax.rem(n5 + 1, 2)).wait()

        @pl.when(tl == 1)
        def _():
            tail_in_dma(e).wait()
            otail[...] = compute(xtail[...]).astype(otail.dtype)
            tail_out.start()
            tail_out.wait()

    last_tile = pl.program_id(1) == pl.num_programs(1) - 1
    nxt = jnp.where(last_tile, e + 1, e)

    @pl.when(nxt < pl.num_programs(0))
    def _():
        start_first_inputs(nxt)

    @pl.when(e == pl.num_programs(0) - 1)
    def _():
        otail[...] = jnp.zeros_like(otail)

        def z_dma(c):
            row = pl.multiple_of(lf_ref[0] + c * ROW_PAD, ROW_PAD)
            return pltpu.make_async_copy(otail, o_hbm.at[pl.ds(row, ROW_PAD), pl.ds(col0, tn_out)], sem_tail.at[1])

        def start(c, carry):
            z_dma(c).start()
            return carry

        def wait(c, carry):
            z_dma(c).wait()
            return carry

        lax.fori_loop(0, lf_ref[1], start, 0)
        lax.fori_loop(0, lf_ref[1], wait, 0)


def _expert_matmul(sched, x, w, b, layer, *, tn, swiglu, out_dtype, vmem_mb, name):
    n_rows, k = x.shape
    n_exp, _, n = w.shape[1:]
    tn_out = tn // 2 if swiglu else tn
    n_out = n // 2 if swiglu else n
    return pl.pallas_call(
        functools.partial(_expert_body, swiglu=swiglu, tn_out=tn_out),
        grid_spec=pltpu.PrefetchScalarGridSpec(
            num_scalar_prefetch=4,
            grid=(n_exp, n // tn),
            in_specs=[pl.BlockSpec(memory_space=pl.ANY),
                      pl.BlockSpec((None, None, k, tn), lambda e, j, n5, tl, r0, lf: (layer, e, 0, j)),
                      pl.BlockSpec((None, None, 1, tn), lambda e, j, n5, tl, r0, lf: (layer, e, 0, j))],
            out_specs=pl.BlockSpec(memory_space=pl.ANY),
            scratch_shapes=[pltpu.VMEM((k, tn), BF16),
                            pltpu.VMEM((2, ROW_BLOCK, k), x.dtype),
                            pltpu.VMEM((2, ROW_BLOCK, tn_out), out_dtype),
                            pltpu.VMEM((ROW_PAD, k), x.dtype),
                            pltpu.VMEM((ROW_PAD, tn_out), out_dtype),
                            pltpu.SemaphoreType.DMA((2,)),
                            pltpu.SemaphoreType.DMA((2,)),
                            pltpu.SemaphoreType.DMA((2,))]),
        out_shape=SDS((n_rows, n_out), out_dtype),
        compiler_params=_cp(("arbitrary", "arbitrary"), vmem_mb),
        name=name,
    )(*sched, x, w, b.reshape(b.shape[0], n_exp, 1, n))


def _moe_experts(h, top_idx, pos, counts, w1, b1, w2, b2, layer, *, tn1=1024, tn2=1024):
    m, d = h.shape
    n_exp = w1.shape[1]
    n_assign = m * TOP_K
    flat_e = top_idx.reshape(-1)
    padded = (counts + ROW_PAD - 1) // ROW_PAD * ROW_PAD
    row0 = (jnp.cumsum(padded) - padded).astype(jnp.int32)
    dest = (row0[flat_e] + pos.reshape(-1)).astype(jnp.int32)
    n_rows = n_assign + n_exp * ROW_PAD
    tok = jnp.arange(n_assign, dtype=jnp.int32) // TOP_K
    row_tok = jnp.zeros((n_rows,), jnp.int32).at[dest].set(tok, mode="promise_in_bounds", unique_indices=True)
    xs = h.at[row_tok].get(mode="promise_in_bounds")
    total = jnp.sum(padded).astype(jnp.int32)
    leftover = jnp.stack([total, (n_rows - total) // ROW_PAD])
    sched = ((padded // ROW_BLOCK).astype(jnp.int32), ((padded // ROW_PAD) % 2).astype(jnp.int32), row0, leftover)
    act = _expert_matmul(sched, xs, w1, b1, layer, tn=tn1, swiglu=True, out_dtype=BF16, vmem_mb=56,
                         name="moe_w1")
    y_rows = _expert_matmul(sched, act, w2, b2, layer, tn=tn2, swiglu=False, out_dtype=F32, vmem_mb=48,
                            name="moe_w2")
    return y_rows, dest


def _combine_body(y0_ref, y1_ref, y2_ref, y3_ref, tg_ref, x_ref, g_ref, gf_ref, o_ref):
    tg = tg_ref[...]
    y = (y0_ref[...] * tg[:, 0:1] + y1_ref[...] * tg[:, 1:2]) + (y2_ref[...] * tg[:, 2:3] + y3_ref[...] * tg[:, 3:4])
    inv = lax.rsqrt(jnp.mean(y * y, axis=-1, keepdims=True) + EPS)
    o_ref[...] = x_ref[...] + gf_ref[...] * (y * inv * g_ref[...])


def _combine(ys, top_gate, x, gpost, gf, grp, *, tm=128):
    m, d = x.shape
    full = pl.BlockSpec((tm, d), lambda i: (i, 0))
    return pl.pallas_call(
        _combine_body,
        grid=(m // tm,),
        in_specs=[full, full, full, full, pl.BlockSpec((tm, LANES), lambda i: (i, 0)), full,
                  pl.BlockSpec((1, d), lambda i: (0, 0)),
                  pl.BlockSpec((None, 1, d), lambda i: (grp(i), 0, 0))],
        out_specs=full,
        out_shape=SDS((m, d), F32),
        compiler_params=_cp(("arbitrary",), 56),
        name="moe_combine",
    )(*ys, top_gate, x, gpost.reshape(1, d), gf)


def _stack_rows_body(a_ref, b_ref, o_ref, sem):
    na = a_ref.shape[0]
    ca = pltpu.make_async_copy(a_ref, o_ref.at[pl.ds(0, na), :], sem.at[0])
    cb = pltpu.make_async_copy(b_ref, o_ref.at[pl.ds(na, b_ref.shape[0]), :], sem.at[1])
    ca.start()
    cb.start()
    ca.wait()
    cb.wait()


def _stack_rows(a, b):
    return pl.pallas_call(
        _stack_rows_body,
        in_specs=[pl.BlockSpec(memory_space=pl.ANY), pl.BlockSpec(memory_space=pl.ANY)],
        out_specs=pl.BlockSpec(memory_space=pl.ANY),
        out_shape=SDS((a.shape[0] + b.shape[0], a.shape[1]), a.dtype),
        scratch_shapes=[pltpu.SemaphoreType.DMA((2,))],
        name="stack_rows",
    )(a, b)


def _prep_w_in(w_in):
    hy_na = 6 * GROUP_W
    mla0 = hy_na
    ckv0 = mla0 + MLA_Q_LORA
    kr0 = ckv0 + MLA_KV_LORA
    ret0 = kr0 + MLA_ROPE
    pad = jnp.zeros(w_in.shape[:2] + (N_PROJ - COL_KROPE - MLA_ROPE,), w_in.dtype)
    parts = [w_in[..., :hy_na], w_in[..., ret0:], w_in[..., ckv0:kr0], w_in[..., mla0:ckv0],
             w_in[..., kr0:ret0], pad]
    return jnp.concatenate(parts, axis=-1).astype(BF16)


def _prep_heads(w, d_a, d_b):
    lead = w.shape[:-1]
    w = w.reshape(lead + (N_HEADS, d_a + d_b))
    a = w[..., :d_a].reshape(lead + (N_HEADS * d_a,))
    b = w[..., d_a:].reshape(lead + (N_HEADS * d_b,))
    return jnp.concatenate([a, b], axis=-1).astype(BF16)


def kernel(x_prompt, x_sample, cache_na_k, cache_na_v, cache_mla_ckv, cache_mla_krope, state_ret, c, c_ctx, w_mod, b_mod, norm_mix_pre, norm_mix_post, norm_ffn_pre, norm_ffn_post, w_in, hy_conv_w, hy_conv_b, hy_filt_w1, hy_filt_b1, hy_filt_w2, hy_filt_b2, hy_filt_w3, hy_bias, na_rpb, mla_q_norm, mla_w_uq, mla_kv_norm, mla_w_ukv, ret_decay, ret_gn, w_out, router_w, router_b, moe_w1, moe_b1, moe_w2, moe_b2):
    nb_c, l_c, d = x_prompt.shape
    nb_l, l_l, _ = x_sample.shape
    depth = w_in.shape[0]
    past = cache_na_k.shape[3]
    m_c = nb_c * l_c
    m_l = nb_l * l_l
    n_exp = router_w.shape[2]

    x = _stack_rows(x_prompt.reshape(m_c, d), x_sample.reshape(m_l, d))
    cv = jnp.zeros((8, d), F32).at[0].set(c_ctx).at[1:1 + nb_l].set(c)

    def grp_fn(tm):
        nc = m_c // tm
        per = l_l // tm
        return lambda i: jnp.where(i < nc, 0, 1 + (i - nc) // per)

    w_in_p = _prep_w_in(w_in)
    w_out_b = w_out.astype(BF16)
    w_uq_p = _prep_heads(mla_w_uq, MLA_NOPE, MLA_ROPE)
    w_ukv_p = _prep_heads(mla_w_ukv, MLA_NOPE, HEAD_DIM)
    rw_p = jnp.pad(router_w, ((0, 0), (0, 0), (0, LANES - n_exp)))
    rw_hi = rw_p.astype(BF16)
    rw_lo = (rw_p - rw_hi.astype(F32)).astype(BF16)
    rb_p = jnp.pad(router_b, ((0, 0), (0, LANES - n_exp)), constant_values=-jnp.inf)
    tables_c = _dft_tables(l_c)
    tables_l = _dft_tables(l_l)
    lg_all = jax.nn.log_sigmoid(ret_decay.astype(F32))
    ckv_cache_rows = cache_mla_ckv.reshape(nb_l * depth * past, MLA_KV_LORA)

    new_k, new_v, new_ckv, new_kr, new_s = [], [], [], [], []
    for layer in range(depth):
        lw = {"hy_conv_w": hy_conv_w[layer], "hy_conv_b": hy_conv_b[layer],
              "hy_filt_w1": hy_filt_w1[layer], "hy_filt_b1": hy_filt_b1[layer],
              "hy_filt_w2": hy_filt_w2[layer], "hy_filt_b2": hy_filt_b2[layer],
              "hy_filt_w3": hy_filt_w3[layer], "hy_bias": hy_bias[layer]}
        mod = _modulation(cv, w_mod, b_mod, layer)
        sa, ca, ga, sf, cf, gf = [a.reshape(8, 1, d) for a in jnp.split(mod, 6, axis=-1)]

        proj = _norm_matmul(x, 0, d, w_in_p[layer], tm=512, tn=512, gain=norm_mix_pre[layer],
                            mod=(ca, sa), grp=grp_fn(512), name="proj_in")

        y_hy_c = _hyena(proj, 0, nb_c, l_c, lw, tables_c)
        y_hy_l = _hyena(proj, m_c, nb_l, l_l, lw, tables_l)

        y_na_c, k_c, v_c = _na_context(proj, nb_c, l_c)
        y_na_l = _na_latent(proj, m_c, nb_l, l_l, cache_na_k, cache_na_v, layer, _na_bias_table(na_rpb[layer]))
        new_k.append(k_c)
        new_v.append(v_c)

        q = _norm_matmul(proj, COL_QC // MLA_Q_LORA, MLA_Q_LORA, w_uq_p[layer], tm=512, tn=512,
                         gain=mla_q_norm[layer], name="mla_q")
        kv, ckv_n = _norm_matmul(proj, COL_CKV // MLA_KV_LORA, MLA_KV_LORA, w_ukv_p[layer], tm=512, tn=1024,
                                 gain=mla_kv_norm[layer], emit_h=True, out_dtype=BF16, name="mla_kv")
        kv_cache = jnp.concatenate(
            [_norm_matmul(ckv_cache_rows, 0, MLA_KV_LORA, w_ukv_p[layer], tm=past, tn=1024, out_dtype=BF16,
                          xrow0=b * depth + layer, m=past, name="mla_kv_cache") for b in range(nb_l)], axis=0)
        new_ckv.append(ckv_n[:m_c].reshape(nb_c, l_c, MLA_KV_LORA))
        kr_c = proj[:m_c, COL_KROPE:COL_KROPE + MLA_ROPE]
        new_kr.append(kr_c.reshape(nb_c, l_c, MLA_ROPE))
        y_mla_c = _mla_attention(
            (q, pl.BlockSpec((l_c, GROUP_W), lambda b, i: (b, 0))),
            (q, pl.BlockSpec((l_c, N_HEADS * MLA_ROPE), lambda b, i: (b, GROUP_W // (N_HEADS * MLA_ROPE)))),
            [((kv, pl.BlockSpec((l_c, GROUP_W), lambda b, i: (b, 0))),
              (proj, pl.BlockSpec((l_c, LANES), lambda b, i: (b, COL_KROPE // LANES))),
              (kv, pl.BlockSpec((l_c, GROUP_W), lambda b, i: (b, 1))))],
            nb=nb_c, lq=l_c, tq=l_c, name="mla_context")
        q_l = q[m_c:]
        qr_l = _axial_rope(q_l[:, GROUP_W:].reshape(nb_l, l_l, N_HEADS, MLA_ROPE).transpose(0, 2, 1, 3))
        qr_l = qr_l.transpose(0, 2, 1, 3).reshape(m_l, N_HEADS * MLA_ROPE)
        kr_l = _axial_rope(proj[m_c:, COL_KROPE:COL_KROPE + MLA_ROPE].reshape(nb_l, l_l, MLA_ROPE))
        kr_l = kr_l.reshape(m_l, MLA_ROPE)
        tq = 256
        nq = l_l // tq
        rb_l = m_c // l_l
        y_mla_l = _mla_attention(
            (q, pl.BlockSpec((tq, GROUP_W), lambda b, i: (m_c // tq + b * nq + i, 0))),
            (qr_l, pl.BlockSpec((tq, N_HEADS * MLA_ROPE), lambda b, i: (b * nq + i, 0))),
            [((kv_cache, pl.BlockSpec((past, GROUP_W), lambda b, i: (b, 0))),
              (cache_mla_krope, pl.BlockSpec((None, None, past, MLA_ROPE), lambda b, i: (b, layer, 0, 0))),
              (kv_cache, pl.BlockSpec((past, GROUP_W), lambda b, i: (b, 1)))),
             ((kv, pl.BlockSpec((l_l, GROUP_W), lambda b, i: (rb_l + b, 0))),
              (kr_l, pl.BlockSpec((l_l, MLA_ROPE), lambda b, i: (b, 0))),
              (kv, pl.BlockSpec((l_l, GROUP_W), lambda b, i: (rb_l + b, 1))))],
            nb=nb_l, lq=l_l, tq=tq, name="mla_latent")

        lg = lg_all[layer]
        y_ret_c, s_c = _retention(proj, 0, nb_c, l_c, l_c, lg, ret_gn[layer], emit_state=True)
        y_ret_l = _retention(proj, m_c, nb_l, l_l, 256, lg, ret_gn[layer], state=state_ret, layer=layer)
        new_s.append(s_c)

        yp = _out_matmul((y_hy_c, y_na_c, y_mla_c, y_ret_c), (y_hy_l, y_na_l, y_mla_l, y_ret_l), w_out_b[layer])
        x1, h_ffn, top_i, top_g, pos, cnt = _post_router(
            yp, x, norm_mix_post[layer], ga, norm_ffn_pre[layer], cf, sf, rw_hi[layer], rw_lo[layer],
            rb_p[layer:layer + 1], grp_fn(256))

        counts = cnt[0, :n_exp].astype(jnp.int32)
        y_rows, dest = _moe_experts(h_ffn, top_i[:, :TOP_K], pos[:, :TOP_K], counts,
                                    moe_w1, moe_b1, moe_w2, moe_b2, layer)
        dest = dest.reshape(m_c + m_l, TOP_K)
        ys = [y_rows.at[dest[:, kk]].get(mode="promise_in_bounds") for kk in range(TOP_K)]
        x = _combine(ys, top_g, x1, norm_ffn_post[layer], gf, grp_fn(128))

    return (x[:m_c].reshape(nb_c, l_c, d), x[m_c:].reshape(nb_l, l_l, d),
            jnp.stack(new_k, axis=1), jnp.stack(new_v, axis=1), jnp.stack(new_ckv, axis=1),
            jnp.stack(new_kr, axis=1), jnp.stack(new_s, axis=1))
```

```python
import functools
import math

import jax
import jax.numpy as jnp
from jax import lax
from jax.experimental import pallas as pl
from jax.experimental.pallas import tpu as pltpu

F32 = jnp.float32
BF16 = jnp.bfloat16
SDS = jax.ShapeDtypeStruct

EPS = 1e-6
GRID_W = 64
HEAD_DIM = 128
N_HEADS = 8
GROUP_W = N_HEADS * HEAD_DIM
HY_BANDS = 16
HY_TARGET = 1e-2
HY_FAST_DECAY = 0.3
HY_SLOW_DECAY = 1.5
NA_WR = 8
NA_WC = 16
MLA_Q_LORA = 768
MLA_KV_LORA = 512
MLA_ROPE = 64
MLA_NOPE = 128
MLA_SCALE = (MLA_NOPE + MLA_ROPE) ** -0.5
ROPE_BASE = 10000.0
TOP_K = 4
SWIGLU_LIMIT = 7.0
SWIGLU_ALPHA = 1.702
LANES = 128

COL_HY = 0
COL_NA = 3072
COL_RET = 6144
COL_CKV = 10240
COL_QC = 10752
COL_KROPE = 11520
N_PROJ = 11776

VMEM_LIMIT_MB = 56


def _cp(sem, vmem_mb=48):
    return pltpu.CompilerParams(dimension_semantics=sem, vmem_limit_bytes=min(vmem_mb, VMEM_LIMIT_MB) * 2**20)


def _dot(a, b):
    return jnp.dot(a, b, preferred_element_type=F32)


def _dot_nt(a, b):
    return lax.dot_general(a, b, (((1,), (1,)), ((), ())), preferred_element_type=F32)


def _dot_tn(a, b):
    return lax.dot_general(a, b, (((0,), (0,)), ((), ())), preferred_element_type=F32)


def _silu(x):
    return x * jax.nn.sigmoid(x)


def _mod_body(c_ref, w_ref, b_ref, o_ref):
    s = _silu(c_ref[...])
    o_ref[...] = _dot(s.astype(BF16), w_ref[...].astype(BF16)) + b_ref[...]


def _modulation(cv, w_mod, b_mod, layer):
    _, d, n = w_mod.shape
    tn = 512
    return pl.pallas_call(
        _mod_body,
        grid=(n // tn,),
        in_specs=[pl.BlockSpec((8, d), lambda j: (0, 0)),
                  pl.BlockSpec((None, d, tn), lambda j: (layer, 0, j)),
                  pl.BlockSpec((None, 1, tn), lambda j: (layer, 0, j))],
        out_specs=pl.BlockSpec((8, tn), lambda j: (0, j)),
        out_shape=SDS((8, n), F32),
        compiler_params=_cp(("arbitrary",), 40),
        name="modulation",
    )(cv, w_mod, b_mod.reshape(b_mod.shape[0], 1, n))


def _norm_matmul_body(*refs, do_norm, has_mod, emit_h):
    it = iter(refs)
    x_ref = next(it)
    g_ref = next(it) if do_norm else None
    sc_ref = next(it) if has_mod else None
    sh_ref = next(it) if has_mod else None
    w_ref = next(it)
    o_ref = next(it)
    h_ref = next(it) if emit_h else None
    hs_ref = next(it)

    @pl.when(pl.program_id(1) == 0)
    def _():
        h = x_ref[...].astype(F32)
        if do_norm:
            h = h * lax.rsqrt(jnp.mean(h * h, axis=-1, keepdims=True) + EPS) * g_ref[...]
        if has_mod:
            h = h * (1.0 + sc_ref[...]) + sh_ref[...]
        if emit_h:
            h_ref[...] = h
        hs_ref[...] = h.astype(BF16)

    o_ref[...] = _dot(hs_ref[...], w_ref[...].astype(BF16)).astype(o_ref.dtype)


def _norm_matmul(x, xcol, k, w, *, tm, tn, gain=None, mod=None, grp=None, emit_h=False,
                 out_dtype=F32, xrow0=0, m=None, name="norm_matmul"):
    m = x.shape[0] if m is None else m
    n = w.shape[1]
    do_norm = gain is not None
    has_mod = mod is not None
    in_specs = [pl.BlockSpec((tm, k), lambda i, j: (i + xrow0, xcol))]
    args = [x]
    if do_norm:
        in_specs.append(pl.BlockSpec((1, k), lambda i, j: (0, 0)))
        args.append(gain.reshape(1, k))
    if has_mod:
        for a in mod:
            in_specs.append(pl.BlockSpec((None, 1, k), lambda i, j: (grp(i), 0, 0)))
            args.append(a)
    in_specs.append(pl.BlockSpec((k, tn), lambda i, j: (0, j)))
    args.append(w)
    out_specs = [pl.BlockSpec((tm, tn), lambda i, j: (i, j))]
    out_shape = [SDS((m, n), out_dtype)]
    if emit_h:
        out_specs.append(pl.BlockSpec((tm, k), lambda i, j: (i, 0)))
        out_shape.append(SDS((m, k), F32))
    res = pl.pallas_call(
        functools.partial(_norm_matmul_body, do_norm=do_norm, has_mod=has_mod, emit_h=emit_h),
        grid=(m // tm, n // tn),
        in_specs=in_specs, out_specs=out_specs, out_shape=out_shape,
        scratch_shapes=[pltpu.VMEM((tm, k), BF16)],
        compiler_params=_cp(("arbitrary", "arbitrary"), 48),
        name=name,
    )(*args)
    return res if emit_h else res[0]


def _dft_tables(l):
    n = 2 * l
    f = jnp.arange(l, dtype=jnp.int32)
    ang = ((f[:, None] * f[None, :]) % n).astype(F32) * (2.0 * math.pi / n)
    fc = jnp.cos(ang)
    alt = jnp.where(f % 2 == 0, 1.0, -1.0).astype(F32)
    fs = jnp.where(f[:, None] == 0, alt[None, :], jnp.sin(ang))
    return fc.astype(BF16), fs.astype(BF16), fs.T.astype(BF16)


def _hy_spec_body(fc_ref, fs_ref, hf_ref, hb_ref, a_ref, b_ref, *, l):
    hf = hf_ref[...]
    row = lax.broadcasted_iota(jnp.int32, hf.shape, 0)
    hb = jnp.where(row == 0, 0.0, hb_ref[...])
    a = _dot(fc_ref[...], (hf + hb).astype(BF16))
    p1 = _dot(fs_ref[...], hf.astype(BF16))
    p2 = _dot(fs_ref[...], hb.astype(BF16))
    orow = lax.broadcasted_iota(jnp.int32, a.shape, 0)
    first = jnp.logical_and(pl.program_id(0) == 0, orow == 0)
    wgt = jnp.where(first, 0.5 / l, 1.0 / l)
    a_ref[...] = wgt * a
    b_ref[...] = wgt * jnp.where(first, p1 + p2, p1 - p2)


def _hy_spectra(fc, fs, hf, hb):
    l, c2 = hf.shape
    tf = min(l, 256)
    tc = 512
    return pl.pallas_call(
        functools.partial(_hy_spec_body, l=l),
        grid=(l // tf, c2 // tc),
        in_specs=[pl.BlockSpec((tf, l), lambda i, j: (i, 0)),
                  pl.BlockSpec((tf, l), lambda i, j: (i, 0)),
                  pl.BlockSpec((l, tc), lambda i, j: (0, j)),
                  pl.BlockSpec((l, tc), lambda i, j: (0, j))],
        out_specs=[pl.BlockSpec((tf, tc), lambda i, j: (i, j))] * 2,
        out_shape=[SDS((l, c2), F32)] * 2,
        compiler_params=_cp(("arbitrary", "arbitrary"), 40),
        name="hyena_filter_spectra",
    )(fc, fs, hf, hb)


def _short_conv(x, w, b):
    l = x.shape[0]
    row = lax.broadcasted_iota(jnp.int32, x.shape, 0)
    xm = jnp.where(row == 0, 0.0, pltpu.roll(x, 1, 0))
    xp = jnp.where(row == l - 1, 0.0, pltpu.roll(x, l - 1, 0))
    return xm * w[0:1, :] + x * w[1:2, :] + xp * w[2:3, :] + b


def _hy_conv_body(*refs, first_order, nf):
    it = iter(refs)
    z_ref = next(it)
    zw_ref = next(it) if first_order else None
    zb_ref = next(it) if first_order else None
    gt_ref, gw_ref, gb_ref, skip_ref = next(it), next(it), next(it), next(it)
    ah_ref, bh_ref, fc_ref, fs_ref, gc_ref, gs_ref = (next(it) for _ in range(6))
    o_ref = next(it)
    zf_scr, zb_scr, acc = next(it), next(it), next(it)
    fb = pl.program_id(2)

    @pl.when(fb == 0)
    def _():
        z = z_ref[...].astype(F32)
        if first_order:
            z = _short_conv(z, zw_ref[...], zb_ref[...])
        zf_scr[...] = z
        zb_scr[...] = z.astype(BF16)
        acc[...] = jnp.zeros_like(acc)

    zb = zb_scr[...]
    uc = _dot(fc_ref[...], zb)
    us = _dot(fs_ref[...], zb)
    ah = ah_ref[...]
    bh = bh_ref[...]
    row = lax.broadcasted_iota(jnp.int32, uc.shape, 0)
    first = jnp.logical_and(fb == 0, row == 0)
    usb = us * bh
    yc = uc * ah - jnp.where(first, 0.0, usb)
    ys = jnp.where(first, usb, uc * bh + us * ah)
    acc[...] += _dot(gc_ref[...], yc.astype(BF16)) + _dot(gs_ref[...], ys.astype(BF16))

    @pl.when(fb == nf - 1)
    def _():
        gate = _short_conv(gt_ref[...].astype(F32), gw_ref[...], gb_ref[...])
        o_ref[...] = (gate * (acc[...] + zf_scr[...] * skip_ref[...])).astype(o_ref.dtype)


def _hy_conv(z, gate_col, proj, rb, conv_w, conv_b, skip, ah, bh, spec_col0, tables, *,
             nb, l, z_col, out_dtype):
    fc, fs, gs = tables
    first_order = z_col is not None
    tc = 512 if l > 512 else GROUP_W
    tf = min(l, 256)
    nf = l // tf

    def proj_specs(col):
        return [pl.BlockSpec((l, tc), lambda b, c, f: (b + rb, col // tc + c)),
                pl.BlockSpec((3, tc), lambda b, c, f: (0, (col - COL_HY) // tc + c)),
                pl.BlockSpec((1, tc), lambda b, c, f: (0, (col - COL_HY) // tc + c))]

    if first_order:
        in_specs = proj_specs(z_col)
        args = [proj, conv_w, conv_b]
    else:
        in_specs = [pl.BlockSpec((l, tc), lambda b, c, f: (b, c))]
        args = [z]
    in_specs += proj_specs(gate_col)
    args += [proj, conv_w, conv_b]
    in_specs += [pl.BlockSpec((1, tc), lambda b, c, f: (0, c)),
                 pl.BlockSpec((tf, tc), lambda b, c, f: (f, spec_col0 // tc + c)),
                 pl.BlockSpec((tf, tc), lambda b, c, f: (f, spec_col0 // tc + c)),
                 pl.BlockSpec((tf, l), lambda b, c, f: (f, 0)),
                 pl.BlockSpec((tf, l), lambda b, c, f: (f, 0)),
                 pl.BlockSpec((l, tf), lambda b, c, f: (0, f)),
                 pl.BlockSpec((l, tf), lambda b, c, f: (0, f))]
    args += [skip, ah, bh, fc, fs, fc, gs]
    return pl.pallas_call(
        functools.partial(_hy_conv_body, first_order=first_order, nf=nf),
        grid=(nb, GROUP_W // tc, nf),
        in_specs=in_specs,
        out_specs=pl.BlockSpec((l, tc), lambda b, c, f: (b, c)),
        out_shape=SDS((nb * l, GROUP_W), out_dtype),
        scratch_shapes=[pltpu.VMEM((l, tc), F32), pltpu.VMEM((l, tc), BF16), pltpu.VMEM((l, tc), F32)],
        compiler_params=_cp(("arbitrary", "arbitrary", "arbitrary"), 52),
        name="hyena_conv",
    )(*args)


def _hyena_filters(l, w1, b1, w2, b2, w3):
    hp = lax.Precision.HIGHEST
    pos = jnp.arange(l, dtype=F32)
    t = pos / l
    bands = jnp.linspace(1e-4, HY_BANDS - 1, HY_BANDS, dtype=F32)
    ang = (2.0 * math.pi * t)[:, None] * bands[None, :]
    z = jnp.concatenate([t[:, None], jnp.cos(ang), -jnp.sin(ang)], axis=-1)
    h = jnp.sin(jnp.dot(z, w1, precision=hp) + b1)
    h = jnp.sin(jnp.dot(h, w2, precision=hp) + b2)
    h = jnp.dot(h, w3, precision=hp)
    deltas = jnp.abs(jnp.linspace(math.log(HY_TARGET) / HY_FAST_DECAY,
                                  math.log(HY_TARGET) / HY_SLOW_DECAY, GROUP_W, dtype=F32))
    window = jnp.exp(-t[:, None] * deltas[None, :])
    filt = h.reshape(l, 2, 2, GROUP_W) * window[:, None, None, :]
    return filt[:, :, 0, :].reshape(l, 2 * GROUP_W), filt[:, :, 1, :].reshape(l, 2 * GROUP_W)


def _hyena(proj, row0, nb, l, lw, tables):
    fc, fs, gs = tables
    hf, hb = _hyena_filters(l, lw["hy_filt_w1"], lw["hy_filt_b1"], lw["hy_filt_w2"], lw["hy_filt_b2"],
                            lw["hy_filt_w3"])
    ah, bh = _hy_spectra(fc, fs, hf, hb)
    cw = lw["hy_conv_w"]
    cb = lw["hy_conv_b"].reshape(1, -1)
    skip = lw["hy_bias"]
    rb = row0 // l
    z1 = _hy_conv(None, COL_HY, proj, rb, cw, cb, skip[0:1], ah, bh, 0, tables,
                  nb=nb, l=l, z_col=COL_HY + 2 * GROUP_W, out_dtype=F32)
    return _hy_conv(z1, COL_HY + GROUP_W, proj, rb, cw, cb, skip[1:2], ah, bh, GROUP_W, tables,
                    nb=nb, l=l, z_col=None, out_dtype=BF16)


def _softmax_attend(scores, values):
    m = functools.reduce(jnp.maximum, [jnp.max(s, axis=-1, keepdims=True) for s in scores])
    ps = [jnp.exp(s - m) for s in scores]
    den = functools.reduce(jnp.add, [jnp.sum(p, axis=-1, keepdims=True) for p in ps])
    o = functools.reduce(jnp.add, [_dot(p.astype(BF16), v) for p, v in zip(ps, values)])
    return o / den


def _na_ctx_body(q_ref, k_ref, v_ref, o_ref, kc_ref, vc_ref):
    scale = HEAD_DIM ** -0.5
    for h in range(N_HEADS):
        hs = slice(h * HEAD_DIM, (h + 1) * HEAD_DIM)
        k = k_ref[:, hs]
        v = v_ref[:, hs]
        kc_ref[h] = k
        vc_ref[h] = v
        s = _dot_nt(q_ref[:, hs].astype(BF16), k.astype(BF16)) * scale
        o_ref[:, hs] = _softmax_attend([s], [v.astype(BF16)]).astype(o_ref.dtype)


def _na_context(proj, nb, l):
    c0 = COL_NA // GROUP_W
    return pl.pallas_call(
        _na_ctx_body,
        grid=(nb,),
        in_specs=[pl.BlockSpec((l, GROUP_W), lambda b: (b, c0)),
                  pl.BlockSpec((l, GROUP_W), lambda b: (b, c0 + 1)),
                  pl.BlockSpec((l, GROUP_W), lambda b: (b, c0 + 2))],
        out_specs=[pl.BlockSpec((l, GROUP_W), lambda b: (b, 0)),
                   pl.BlockSpec((None, N_HEADS, l, HEAD_DIM), lambda b: (b, 0, 0, 0)),
                   pl.BlockSpec((None, N_HEADS, l, HEAD_DIM), lambda b: (b, 0, 0, 0))],
        out_shape=[SDS((nb * l, GROUP_W), BF16),
                   SDS((nb, N_HEADS, l, HEAD_DIM), F32),
                   SDS((nb, N_HEADS, l, HEAD_DIM), F32)],
        compiler_params=_cp(("arbitrary",), 32),
        name="na_context",
    )(proj, proj, proj)


def _na_bias_table(rpb):
    cidx = jnp.arange(GRID_W)
    col_start = jnp.clip(cidx - NA_WC // 2, 0, GRID_W - NA_WC)
    col_in = (cidx[None, :] >= col_start[:, None]) & (cidx[None, :] < col_start[:, None] + NA_WC)
    col_off = jnp.clip(cidx[None, :] - cidx[:, None], -(NA_WC - 1), NA_WC - 1) + (NA_WC - 1)
    t = rpb[:, :, col_off].astype(F32)
    t = jnp.where(col_in[None, None], t, -jnp.inf)
    return t.transpose(0, 2, 1, 3).reshape(rpb.shape[0], GRID_W, -1)


def _na_lat_body(q_ref, k_ref, v_ref, kc_ref, vc_ref, bt_ref, o_ref, *, rows):
    scale = HEAD_DIM ** -0.5
    wr = min(NA_WR, rows)
    kcx = kc_ref[...].astype(BF16)
    vcx = vc_ref[...].astype(BF16)
    nloc = wr * GRID_W
    bt = bt_ref[...]
    for r in range(rows):
        rs = min(max(r - wr // 2, 0), rows - wr)
        off = rs - r + (NA_WR - 1)
        q = q_ref[r * GRID_W:(r + 1) * GRID_W, :].astype(BF16)
        kl = k_ref[rs * GRID_W:rs * GRID_W + nloc, :].astype(BF16)
        vl = v_ref[rs * GRID_W:rs * GRID_W + nloc, :].astype(BF16)
        s_loc = _dot_nt(q, kl) * scale + bt[:, off * GRID_W:off * GRID_W + nloc]
        s_ctx = _dot_nt(q, kcx) * scale
        o_ref[r * GRID_W:(r + 1) * GRID_W, :] = _softmax_attend([s_loc, s_ctx], [vl, vcx]).astype(o_ref.dtype)


def _na_latent(proj, row0, nb, l, cache_k, cache_v, layer, bias_table):
    rows = l // GRID_W
    rb = row0 // l
    c0 = COL_NA // HEAD_DIM
    past = cache_k.shape[3]
    cspec = pl.BlockSpec((None, None, None, past, HEAD_DIM), lambda b, h: (b, layer, h, 0, 0))
    return pl.pallas_call(
        functools.partial(_na_lat_body, rows=rows),
        grid=(nb, N_HEADS),
        in_specs=[pl.BlockSpec((l, HEAD_DIM), lambda b, h: (b + rb, c0 + h)),
                  pl.BlockSpec((l, HEAD_DIM), lambda b, h: (b + rb, c0 + N_HEADS + h)),
                  pl.BlockSpec((l, HEAD_DIM), lambda b, h: (b + rb, c0 + 2 * N_HEADS + h)),
                  cspec, cspec,
                  pl.BlockSpec((None, GRID_W, bias_table.shape[2]), lambda b, h: (h, 0, 0))],
        out_specs=pl.BlockSpec((l, HEAD_DIM), lambda b, h: (b, h)),
        out_shape=SDS((nb * l, GROUP_W), BF16),
        compiler_params=_cp(("arbitrary", "arbitrary"), 32),
        name="na_latent",
    )(proj, proj, proj, cache_k, cache_v, bias_table)


def _mla_attn_body(*refs, nseg):
    qn_ref, qr_ref = refs[0], refs[1]
    segs = [refs[2 + 3 * i: 5 + 3 * i] for i in range(nseg)]
    o_ref = refs[2 + 3 * nseg]
    krs = [kr_ref[...][:, :MLA_ROPE].astype(BF16) for _, kr_ref, _ in segs]
    qr_all = qr_ref[...].astype(BF16)
    for h in range(N_HEADS):
        hs = slice(h * HEAD_DIM, (h + 1) * HEAD_DIM)
        qn = qn_ref[:, hs].astype(BF16)
        qr = qr_all[:, h * MLA_ROPE:(h + 1) * MLA_ROPE]
        scores = [(_dot_nt(qn, kn_ref[:, hs].astype(BF16)) + _dot_nt(qr, kr)) * MLA_SCALE
                  for (kn_ref, _, _), kr in zip(segs, krs)]
        vals = [v_ref[:, hs].astype(BF16) for _, _, v_ref in segs]
        o_ref[:, hs] = _softmax_attend(scores, vals).astype(o_ref.dtype)


def _mla_attention(qn, qr, segs, *, nb, lq, tq, name):
    arrays = [qn[0], qr[0]]
    specs = [qn[1], qr[1]]
    for s in segs:
        for a, sp in s:
            arrays.append(a)
            specs.append(sp)
    return pl.pallas_call(
        functools.partial(_mla_attn_body, nseg=len(segs)),
        grid=(nb, lq // tq),
        in_specs=specs,
        out_specs=pl.BlockSpec((tq, GROUP_W), lambda b, i: (b * (lq // tq) + i, 0)),
        out_shape=SDS((nb * lq, GROUP_W), BF16),
        compiler_params=_cp(("arbitrary", "arbitrary"), 48),
        name=name,
    )(*arrays)


def _axial_rope(x):
    l = x.shape[-2]
    pos = jnp.arange(l)
    half = MLA_ROPE // 2
    freqs = jnp.power(ROPE_BASE, -jnp.arange(0, half, 2, dtype=F32) / half)

    def rot(xa, p):
        ang = p.astype(F32)[:, None] * freqs[None, :]
        cos, sin = jnp.cos(ang), jnp.sin(ang)
        a1, a2 = jnp.split(xa, 2, axis=-1)
        return jnp.concatenate([a1 * cos - a2 * sin, a2 * cos + a1 * sin], axis=-1)

    return jnp.concatenate([rot(x[..., :half], pos // GRID_W), rot(x[..., half:], pos % GRID_W)], axis=-1)


def _ret_body(*refs, l, tq, has_state, emit_state):
    it = iter(refs)
    lg_ref = next(it)
    q_ref, k_ref, v_ref, g_ref, gn_ref = (next(it) for _ in range(5))
    s0_ref = next(it) if has_state else None
    o_ref = next(it)
    so_ref = next(it) if emit_state else None
    t0 = pl.program_id(1) * tq
    ti = lax.broadcasted_iota(jnp.int32, (tq, l), 0) + t0
    si = lax.broadcasted_iota(jnp.int32, (tq, l), 1)
    d = (ti - si).astype(F32)
    tcol = (lax.broadcasted_iota(jnp.int32, (tq, 1), 0) + t0).astype(F32)
    scol = lax.broadcasted_iota(jnp.int32, (l, 1), 0).astype(F32)
    kscale = HEAD_DIM ** -0.5
    for h in range(N_HEADS):
        hs = slice(h * HEAD_DIM, (h + 1) * HEAD_DIM)
        lgf = lg_ref[0, h]
        lgb = lg_ref[1, h]
        q = q_ref[:, hs]
        k = k_ref[:, hs] * kscale
        v = v_ref[:, hs].astype(BF16)
        s = _dot_nt(q.astype(BF16), k.astype(BF16))
        w = jnp.exp(jnp.where(d > 0, lgf, -lgb) * d)
        w = jnp.where(d == 0, 2.0, w)
        o = _dot((s * w).astype(BF16), v)
        if has_state:
            qf = q * jnp.exp(lgf * (tcol + 1.0))
            qb = q * jnp.exp(lgb * (l - tcol))
            o = o + _dot(qf.astype(BF16), s0_ref[0, h].astype(BF16)) + _dot(qb.astype(BF16), s0_ref[1, h].astype(BF16))
        o = o * lax.rsqrt(jnp.mean(o * o, axis=-1, keepdims=True) + EPS) * gn_ref[:, hs]
        o_ref[:, hs] = (o * _silu(g_ref[:, hs])).astype(o_ref.dtype)
        if emit_state:
            kf = k * jnp.exp(lgf * (l - 1.0 - scol))
            kb = k * jnp.exp(lgb * scol)
            so_ref[0, h] = _dot_tn(kf.astype(BF16), v)
            so_ref[1, h] = _dot_tn(kb.astype(BF16), v)


def _retention(proj, row0, nb, l, tq, lg, gn, state=None, layer=0, emit_state=False):
    assert not emit_state or tq == l
    nq = l // tq
    rbq = row0 // tq
    rbl = row0 // l
    c0 = COL_RET // GROUP_W
    in_specs = [pl.BlockSpec(memory_space=pltpu.SMEM),
                pl.BlockSpec((tq, GROUP_W), lambda b, i: (rbq + b * nq + i, c0)),
                pl.BlockSpec((l, GROUP_W), lambda b, i: (rbl + b, c0 + 1)),
                pl.BlockSpec((l, GROUP_W), lambda b, i: (rbl + b, c0 + 2)),
                pl.BlockSpec((tq, GROUP_W), lambda b, i: (rbq + b * nq + i, c0 + 3)),
                pl.BlockSpec((1, GROUP_W), lambda b, i: (0, 0))]
    args = [lg, proj, proj, proj, proj, gn.reshape(1, GROUP_W)]
    if state is not None:
        in_specs.append(pl.BlockSpec((None, None, 2, N_HEADS, HEAD_DIM, HEAD_DIM),
                                     lambda b, i: (b, layer, 0, 0, 0, 0)))
        args.append(state)
    out_specs = [pl.BlockSpec((tq, GROUP_W), lambda b, i: (b * nq + i, 0))]
    out_shape = [SDS((nb * l, GROUP_W), BF16)]
    if emit_state:
        out_specs.append(pl.BlockSpec((None, 2, N_HEADS, HEAD_DIM, HEAD_DIM), lambda b, i: (b, 0, 0, 0, 0)))
        out_shape.append(SDS((nb, 2, N_HEADS, HEAD_DIM, HEAD_DIM), F32))
    res = pl.pallas_call(
        functools.partial(_ret_body, l=l, tq=tq, has_state=state is not None, emit_state=emit_state),
        grid=(nb, nq),
        in_specs=in_specs, out_specs=out_specs, out_shape=out_shape,
        compiler_params=_cp(("arbitrary", "arbitrary"), 48),
        name="retention",
    )(*args)
    return res if emit_state else res[0]


def _out_matmul_body(*refs, nc):
    ctx, lat, w_ref, o_ref = refs[0:4], refs[4:8], refs[8], refs[9]

    def project(ys):
        acc = _dot(ys[0][...], w_ref[0 * GROUP_W:1 * GROUP_W, :])
        acc += _dot(ys[1][...], w_ref[1 * GROUP_W:2 * GROUP_W, :])
        acc += _dot(ys[2][...], w_ref[2 * GROUP_W:3 * GROUP_W, :])
        acc += _dot(ys[3][...], w_ref[3 * GROUP_W:4 * GROUP_W, :])
        o_ref[...] = acc

    @pl.when(pl.program_id(1) < nc)
    def _():
        project(ctx)

    @pl.when(pl.program_id(1) >= nc)
    def _():
        project(lat)


def _out_matmul(ys_ctx, ys_lat, w_out, *, tm=512, tn=1024):
    m_c = ys_ctx[0].shape[0]
    m = m_c + ys_lat[0].shape[0]
    nc = m_c // tm
    d = w_out.shape[1]
    cspec = pl.BlockSpec((tm, GROUP_W), lambda j, i: (jnp.minimum(i, nc - 1), 0))
    lspec = pl.BlockSpec((tm, GROUP_W), lambda j, i: (jnp.maximum(i - nc, 0), 0))
    return pl.pallas_call(
        functools.partial(_out_matmul_body, nc=nc),
        grid=(d // tn, m // tm),
        in_specs=[cspec] * 4 + [lspec] * 4 + [pl.BlockSpec((4 * GROUP_W, tn), lambda j, i: (0, j))],
        out_specs=pl.BlockSpec((tm, tn), lambda j, i: (i, j)),
        out_shape=SDS((m, d), F32),
        compiler_params=_cp(("arbitrary", "arbitrary"), 48),
        name="out_proj",
    )(*ys_ctx, *ys_lat, w_out)


def _topk(logits):
    lane = lax.broadcasted_iota(jnp.int32, logits.shape, 1)
    work = logits
    idx_out = jnp.zeros(logits.shape, jnp.int32)
    val_out = jnp.full(logits.shape, -jnp.inf, F32)
    chosen = jnp.zeros(logits.shape, F32)
    idxs = []
    for k in range(TOP_K):
        m = jnp.max(work, axis=-1, keepdims=True)
        idx = jnp.min(jnp.where(work == m, lane, LANES), axis=-1, keepdims=True)
        idx_out = jnp.where(lane == k, idx, idx_out)
        val_out = jnp.where(lane == k, m, val_out)
        chosen = jnp.where(lane == idx, 1.0, chosen)
        work = jnp.where(lane == idx, -jnp.inf, work)
        idxs.append(idx)
    e = jnp.exp(val_out - jnp.max(val_out, axis=-1, keepdims=True))
    return idx_out, e / jnp.sum(e, axis=-1, keepdims=True), chosen, idxs


def _post_router_body(yp_ref, x_ref, gpost_ref, ga_ref, gpre_ref, cf_ref, sf_ref, rwh_ref, rwl_ref, rb_ref,
                      x1_ref, h_ref, ti_ref, tg_ref, pos_ref, cnt_ref, carry):
    @pl.when(pl.program_id(0) == 0)
    def _():
        carry[...] = jnp.zeros_like(carry)

    y = yp_ref[...]
    y = y * lax.rsqrt(jnp.mean(y * y, axis=-1, keepdims=True) + EPS) * gpost_ref[...]
    x1 = x_ref[...] + ga_ref[...] * y
    x1_ref[...] = x1
    h = x1 * lax.rsqrt(jnp.mean(x1 * x1, axis=-1, keepdims=True) + EPS) * gpre_ref[...]
    h = h * (1.0 + cf_ref[...]) + sf_ref[...]
    h_ref[...] = h.astype(h_ref.dtype)
    h_hi = h.astype(BF16)
    h_lo = (h - h_hi.astype(F32)).astype(BF16)
    logits = _dot(h_hi, rwh_ref[...]) + _dot(h_lo, rwh_ref[...]) + _dot(h_hi, rwl_ref[...]) + rb_ref[...]
    idx_out, gates, chosen, idxs = _topk(logits)
    ti_ref[...] = idx_out
    tg_ref[...] = gates
    tm = chosen.shape[0]
    r = lax.broadcasted_iota(jnp.int32, (tm, tm), 0)
    c = lax.broadcasted_iota(jnp.int32, (tm, tm), 1)
    before = _dot(jnp.where(c < r, 1.0, 0.0).astype(BF16), chosen.astype(BF16)) + carry[0:1, :]
    lane = lax.broadcasted_iota(jnp.int32, chosen.shape, 1)
    pos = jnp.zeros(chosen.shape, jnp.int32)
    for k, idx in enumerate(idxs):
        pk = jnp.sum(jnp.where(lane == idx, before, 0.0), axis=-1, keepdims=True)
        pos = jnp.where(lane == k, pk.astype(jnp.int32), pos)
    pos_ref[...] = pos
    carry[...] = carry[...] + jnp.sum(chosen, axis=0, keepdims=True)
    cnt_ref[...] = carry[...]


def _post_router(yp, x, gpost, ga, gpre, cf, sf, rw_hi, rw_lo, rb, grp, *, tm=256):
    m, d = x.shape
    row = pl.BlockSpec((1, d), lambda i: (0, 0))
    modspec = pl.BlockSpec((None, 1, d), lambda i: (grp(i), 0, 0))
    full = pl.BlockSpec((tm, d), lambda i: (i, 0))
    small = pl.BlockSpec((tm, LANES), lambda i: (i, 0))
    rwspec = pl.BlockSpec((d, LANES), lambda i: (0, 0))
    return pl.pallas_call(
        _post_router_body,
        grid=(m // tm,),
        in_specs=[full, full, row, modspec, row, modspec, modspec, rwspec, rwspec,
                  pl.BlockSpec((1, LANES), lambda i: (0, 0))],
        out_specs=[full, full, small, small, small, pl.BlockSpec((8, LANES), lambda i: (0, 0))],
        out_shape=[SDS((m, d), F32), SDS((m, d), BF16), SDS((m, LANES), jnp.int32), SDS((m, LANES), F32),
                   SDS((m, LANES), jnp.int32), SDS((8, LANES), F32)],
        scratch_shapes=[pltpu.VMEM((8, LANES), F32)],
        compiler_params=_cp(("arbitrary",), 52),
        name="post_norm_router",
    )(yp, x, gpost.reshape(1, d), ga, gpre.reshape(1, d), cf, sf, rw_hi, rw_lo, rb)


def _deinterleave(hh):
    tm, tn = hh.shape
    lane = lax.broadcasted_iota(jnp.int32, (tm, LANES), 1)
    half = LANES // 2
    idx = jnp.where(lane < half, 2 * lane, 2 * (lane - half) + 1)
    ev, od = [], []
    for c in range(tn // (2 * LANES)):
        g0 = jnp.take_along_axis(hh[:, (2 * c) * LANES:(2 * c + 1) * LANES], idx, axis=1)
        g1 = jnp.take_along_axis(hh[:, (2 * c + 1) * LANES:(2 * c + 2) * LANES], idx, axis=1)
        ev.append(jnp.where(lane < half, g0, pltpu.roll(g1, half, 1)))
        od.append(jnp.where(lane < half, pltpu.roll(g0, half, 1), g1))
    return jnp.concatenate(ev, axis=1), jnp.concatenate(od, axis=1)


def _cast_weights(w_ref, wb_ref, chunk=256):
    k = w_ref.shape[0]

    def step(c, carry):
        r = pl.multiple_of(c * chunk, chunk)
        wb_ref[pl.ds(r, chunk), :] = w_ref[pl.ds(r, chunk), :].astype(BF16)
        return carry

    lax.fori_loop(0, k // chunk, step, 0)


ROW_BLOCK = 512
ROW_PAD = 256


def _swiglu(hh):
    glu, lin = _deinterleave(hh)
    glu = jnp.minimum(glu, SWIGLU_LIMIT)
    lin = jnp.clip(lin, -SWIGLU_LIMIT, SWIGLU_LIMIT)
    return glu * jax.nn.sigmoid(SWIGLU_ALPHA * glu) * (lin + 1.0)


def _expert_body(n5_ref, tl_ref, r0_ref, lf_ref, x_hbm, w_ref, b_ref, o_hbm, wb, xbuf, obuf, xtail, otail,
                 sem_in, sem_out, sem_tail, *, swiglu, tn_out):
    e = pl.program_id(0)
    n5 = n5_ref[e]
    tl = tl_ref[e]
    r0 = r0_ref[e]
    col0 = pl.multiple_of(pl.program_id(1) * tn_out, tn_out)

    def x_dma(b, slot):
        row = pl.multiple_of(r0 + b * ROW_BLOCK, ROW_PAD)
        return pltpu.make_async_copy(x_hbm.at[pl.ds(row, ROW_BLOCK), :], xbuf.at[slot], sem_in.at[slot])

    def o_dma(b, slot):
        row = pl.multiple_of(r0 + b * ROW_BLOCK, ROW_PAD)
        return pltpu.make_async_copy(obuf.at[slot], o_hbm.at[pl.ds(row, ROW_BLOCK), pl.ds(col0, tn_out)],
                                     sem_out.at[slot])

    def tail_in_dma(ee):
        row = pl.multiple_of(r0_ref[ee] + n5_ref[ee] * ROW_BLOCK, ROW_PAD)
        return pltpu.make_async_copy(x_hbm.at[pl.ds(row, ROW_PAD), :], xtail, sem_tail.at[0])

    def start_first_inputs(ee):
        @pl.when(n5_ref[ee] > 0)
        def _():
            row = pl.multiple_of(r0_ref[ee], ROW_PAD)
            pltpu.make_async_copy(x_hbm.at[pl.ds(row, ROW_BLOCK), :], xbuf.at[0], sem_in.at[0]).start()

        @pl.when(tl_ref[ee] == 1)
        def _():
            tail_in_dma(ee).start()

    trow = pl.multiple_of(r0 + n5 * ROW_BLOCK, ROW_PAD)
    tail_out = pltpu.make_async_copy(otail, o_hbm.at[pl.ds(trow, ROW_PAD), pl.ds(col0, tn_out)], sem_tail.at[1])

    def compute(x):
        hh = _dot(x, wb[...]) + b_ref[...]
        return _swiglu(hh) if swiglu else hh

    @pl.when(jnp.logical_and(e == 0, pl.program_id(1) == 0))
    def _():
        start_first_inputs(0)

    @pl.when(n5 + tl > 0)
    def _():
        _cast_weights(w_ref, wb)

        def loop(b, carry):
            slot = lax.rem(b, 2)
            x_dma(b, slot).wait()

            @pl.when(b + 1 < n5)
            def _():
                x_dma(b + 1, 1 - slot).start()

            @pl.when(b >= 2)
            def _():
                o_dma(b - 2, slot).wait()

            obuf[slot] = compute(xbuf[slot]).astype(obuf.dtype)
            o_dma(b, slot).start()
            return carry

        lax.fori_loop(0, n5, loop, 0)

        @pl.when(n5 >= 2)
        def _():
            o_dma(n5 - 2, lax.rem(n5, 2)).wait()

        @pl.when(n5 >= 1)
        def _():
            o_dma(n5 - 1, lax.rem(n5 + 1, 2)).wait()

        @pl.when(tl == 1)
        def _():
            tail_in_dma(e).wait()
            otail[...] = compute(xtail[...]).astype(otail.dtype)
            tail_out.start()
            tail_out.wait()

    last_tile = pl.program_id(1) == pl.num_programs(1) - 1
    nxt = jnp.where(last_tile, e + 1, e)

    @pl.when(nxt < pl.num_programs(0))
    def _():
        start_first_inputs(nxt)

    @pl.when(e == pl.num_programs(0) - 1)
    def _():
        otail[...] = jnp.zeros_like(otail)

        def z_dma(c):
            row = pl.multiple_of(lf_ref[0] + c * ROW_PAD, ROW_PAD)
            return pltpu.make_async_copy(otail, o_hbm.at[pl.ds(row, ROW_PAD), pl.ds(col0, tn_out)], sem_tail.at[1])

        def start(c, carry):
            z_dma(c).start()
            return carry

        def wait(c, carry):
            z_dma(c).wait()
            return carry

        lax.fori_loop(0, lf_ref[1], start, 0)
        lax.fori_loop(0, lf_ref[1], wait, 0)


def _expert_matmul(sched, x, w, b, layer, *, tn, swiglu, out_dtype, vmem_mb, name):
    n_rows, k = x.shape
    n_exp, _, n = w.shape[1:]
    tn_out = tn // 2 if swiglu else tn
    n_out = n // 2 if swiglu else n
    return pl.pallas_call(
        functools.partial(_expert_body, swiglu=swiglu, tn_out=tn_out),
        grid_spec=pltpu.PrefetchScalarGridSpec(
            num_scalar_prefetch=4,
            grid=(n_exp, n // tn),
            in_specs=[pl.BlockSpec(memory_space=pl.ANY),
                      pl.BlockSpec((None, None, k, tn), lambda e, j, n5, tl, r0, lf: (layer, e, 0, j)),
                      pl.BlockSpec((None, None, 1, tn), lambda e, j, n5, tl, r0, lf: (layer, e, 0, j))],
            out_specs=pl.BlockSpec(memory_space=pl.ANY),
            scratch_shapes=[pltpu.VMEM((k, tn), BF16),
                            pltpu.VMEM((2, ROW_BLOCK, k), x.dtype),
                            pltpu.VMEM((2, ROW_BLOCK, tn_out), out_dtype),
                            pltpu.VMEM((ROW_PAD, k), x.dtype),
                            pltpu.VMEM((ROW_PAD, tn_out), out_dtype),
                            pltpu.SemaphoreType.DMA((2,)),
                            pltpu.SemaphoreType.DMA((2,)),
                            pltpu.SemaphoreType.DMA((2,))]),
        out_shape=SDS((n_rows, n_out), out_dtype),
        compiler_params=_cp(("arbitrary", "arbitrary"), vmem_mb),
        name=name,
    )(*sched, x, w, b.reshape(b.shape[0], n_exp, 1, n))


def _moe_experts(h, top_idx, pos, counts, w1, b1, w2, b2, layer, *, tn1=1024, tn2=1024):
    m, d = h.shape
    n_exp = w1.shape[1]
    n_assign = m * TOP_K
    flat_e = top_idx.reshape(-1)
    padded = (counts + ROW_PAD - 1) // ROW_PAD * ROW_PAD
    row0 = (jnp.cumsum(padded) - padded).astype(jnp.int32)
    dest = (row0[flat_e] + pos.reshape(-1)).astype(jnp.int32)
    n_rows = n_assign + n_exp * ROW_PAD
    tok = jnp.arange(n_assign, dtype=jnp.int32) // TOP_K
    row_tok = jnp.zeros((n_rows,), jnp.int32).at[dest].set(tok, mode="promise_in_bounds", unique_indices=True)
    xs = h.at[row_tok].get(mode="promise_in_bounds")
    total = jnp.sum(padded).astype(jnp.int32)
    leftover = jnp.stack([total, (n_rows - total) // ROW_PAD])
    sched = ((padded // ROW_BLOCK).astype(jnp.int32), ((padded // ROW_PAD) % 2).astype(jnp.int32), row0, leftover)
    act = _expert_matmul(sched, xs, w1, b1, layer, tn=tn1, swiglu=True, out_dtype=BF16, vmem_mb=56,
                         name="moe_w1")
    y_rows = _expert_matmul(sched, act, w2, b2, layer, tn=tn2, swiglu=False, out_dtype=F32, vmem_mb=48,
                            name="moe_w2")
    return y_rows, dest


def _combine_body(y0_ref, y1_ref, y2_ref, y3_ref, tg_ref, x_ref, g_ref, gf_ref, o_ref):
    tg = tg_ref[...]
    y = (y0_ref[...] * tg[:, 0:1] + y1_ref[...] * tg[:, 1:2]) + (y2_ref[...] * tg[:, 2:3] + y3_ref[...] * tg[:, 3:4])
    inv = lax.rsqrt(jnp.mean(y * y, axis=-1, keepdims=True) + EPS)
    o_ref[...] = x_ref[...] + gf_ref[...] * (y * inv * g_ref[...])


def _combine(ys, top_gate, x, gpost, gf, grp, *, tm=128):
    m, d = x.shape
    full = pl.BlockSpec((tm, d), lambda i: (i, 0))
    return pl.pallas_call(
        _combine_body,
        grid=(m // tm,),
        in_specs=[full, full, full, full, pl.BlockSpec((tm, LANES), lambda i: (i, 0)), full,
                  pl.BlockSpec((1, d), lambda i: (0, 0)),
                  pl.BlockSpec((None, 1, d), lambda i: (grp(i), 0, 0))],
        out_specs=full,
        out_shape=SDS((m, d), F32),
        compiler_params=_cp(("arbitrary",), 56),
        name="moe_combine",
    )(*ys, top_gate, x, gpost.reshape(1, d), gf)


def _prep_w_in(w_in):
    hy_na = 6 * GROUP_W
    mla0 = hy_na
    ckv0 = mla0 + MLA_Q_LORA
    kr0 = ckv0 + MLA_KV_LORA
    ret0 = kr0 + MLA_ROPE
    pad = jnp.zeros(w_in.shape[:2] + (N_PROJ - COL_KROPE - MLA_ROPE,), w_in.dtype)
    parts = [w_in[..., :hy_na], w_in[..., ret0:], w_in[..., ckv0:kr0], w_in[..., mla0:ckv0],
             w_in[..., kr0:ret0], pad]
    return jnp.concatenate(parts, axis=-1).astype(BF16)


def _prep_heads(w, d_a, d_b):
    lead = w.shape[:-1]
    w = w.reshape(lead + (N_HEADS, d_a + d_b))
    a = w[..., :d_a].reshape(lead + (N_HEADS * d_a,))
    b = w[..., d_a:].reshape(lead + (N_HEADS * d_b,))
    return jnp.concatenate([a, b], axis=-1).astype(BF16)


def kernel(x_prompt, x_sample, cache_na_k, cache_na_v, cache_mla_ckv, cache_mla_krope, state_ret, c, c_ctx, w_mod, b_mod, norm_mix_pre, norm_mix_post, norm_ffn_pre, norm_ffn_post, w_in, hy_conv_w, hy_conv_b, hy_filt_w1, hy_filt_b1, hy_filt_w2, hy_filt_b2, hy_filt_w3, hy_bias, na_rpb, mla_q_norm, mla_w_uq, mla_kv_norm, mla_w_ukv, ret_decay, ret_gn, w_out, router_w, router_b, moe_w1, moe_b1, moe_w2, moe_b2):
    nb_c, l_c, d = x_prompt.shape
    nb_l, l_l, _ = x_sample.shape
    depth = w_in.shape[0]
    past = cache_na_k.shape[3]
    m_c = nb_c * l_c
    m_l = nb_l * l_l
    n_exp = router_w.shape[2]

    x = jnp.concatenate([x_prompt.reshape(m_c, d), x_sample.reshape(m_l, d)], axis=0)
    cv = jnp.zeros((8, d), F32).at[0].set(c_ctx).at[1:1 + nb_l].set(c)

    def grp_fn(tm):
        nc = m_c // tm
        per = l_l // tm
        return lambda i: jnp.where(i < nc, 0, 1 + (i - nc) // per)

    w_in_p = _prep_w_in(w_in)
    w_out_b = w_out.astype(BF16)
    w_uq_p = _prep_heads(mla_w_uq, MLA_NOPE, MLA_ROPE)
    w_ukv_p = _prep_heads(mla_w_ukv, MLA_NOPE, HEAD_DIM)
    rw_p = jnp.pad(router_w, ((0, 0), (0, 0), (0, LANES - n_exp)))
    rw_hi = rw_p.astype(BF16)
    rw_lo = (rw_p - rw_hi.astype(F32)).astype(BF16)
    rb_p = jnp.pad(router_b, ((0, 0), (0, LANES - n_exp)), constant_values=-jnp.inf)
    tables_c = _dft_tables(l_c)
    tables_l = _dft_tables(l_l)
    lg_all = jax.nn.log_sigmoid(ret_decay.astype(F32))
    ckv_cache_rows = cache_mla_ckv.reshape(nb_l * depth * past, MLA_KV_LORA)

    new_k, new_v, new_ckv, new_kr, new_s = [], [], [], [], []
    for layer in range(depth):
        lw = {"hy_conv_w": hy_conv_w[layer], "hy_conv_b": hy_conv_b[layer],
              "hy_filt_w1": hy_filt_w1[layer], "hy_filt_b1": hy_filt_b1[layer],
              "hy_filt_w2": hy_filt_w2[layer], "hy_filt_b2": hy_filt_b2[layer],
              "hy_filt_w3": hy_filt_w3[layer], "hy_bias": hy_bias[layer]}
        mod = _modulation(cv, w_mod, b_mod, layer)
        sa, ca, ga, sf, cf, gf = [a.reshape(8, 1, d) for a in jnp.split(mod, 6, axis=-1)]

        proj = _norm_matmul(x, 0, d, w_in_p[layer], tm=512, tn=512, gain=norm_mix_pre[layer],
                            mod=(ca, sa), grp=grp_fn(512), name="proj_in")

        y_hy_c = _hyena(proj, 0, nb_c, l_c, lw, tables_c)
        y_hy_l = _hyena(proj, m_c, nb_l, l_l, lw, tables_l)

        y_na_c, k_c, v_c = _na_context(proj, nb_c, l_c)
        y_na_l = _na_latent(proj, m_c, nb_l, l_l, cache_na_k, cache_na_v, layer, _na_bias_table(na_rpb[layer]))
        new_k.append(k_c)
        new_v.append(v_c)

        q = _norm_matmul(proj, COL_QC // MLA_Q_LORA, MLA_Q_LORA, w_uq_p[layer], tm=512, tn=512,
                         gain=mla_q_norm[layer], name="mla_q")
        kv, ckv_n = _norm_matmul(proj, COL_CKV // MLA_KV_LORA, MLA_KV_LORA, w_ukv_p[layer], tm=512, tn=1024,
                                 gain=mla_kv_norm[layer], emit_h=True, out_dtype=BF16, name="mla_kv")
        kv_cache = jnp.concatenate(
            [_norm_matmul(ckv_cache_rows, 0, MLA_KV_LORA, w_ukv_p[layer], tm=past, tn=1024, out_dtype=BF16,
                          xrow0=b * depth + layer, m=past, name="mla_kv_cache") for b in range(nb_l)], axis=0)
        new_ckv.append(ckv_n[:m_c].reshape(nb_c, l_c, MLA_KV_LORA))
        kr_c = proj[:m_c, COL_KROPE:COL_KROPE + MLA_ROPE]
        new_kr.append(kr_c.reshape(nb_c, l_c, MLA_ROPE))
        y_mla_c = _mla_attention(
            (q, pl.BlockSpec((l_c, GROUP_W), lambda b, i: (b, 0))),
            (q, pl.BlockSpec((l_c, N_HEADS * MLA_ROPE), lambda b, i: (b, GROUP_W // (N_HEADS * MLA_ROPE)))),
            [((kv, pl.BlockSpec((l_c, GROUP_W), lambda b, i: (b, 0))),
              (proj, pl.BlockSpec((l_c, LANES), lambda b, i: (b, COL_KROPE // LANES))),
              (kv, pl.BlockSpec((l_c, GROUP_W), lambda b, i: (b, 1))))],
            nb=nb_c, lq=l_c, tq=l_c, name="mla_context")
        q_l = q[m_c:]
        qr_l = _axial_rope(q_l[:, GROUP_W:].reshape(nb_l, l_l, N_HEADS, MLA_ROPE).transpose(0, 2, 1, 3))
        qr_l = qr_l.transpose(0, 2, 1, 3).reshape(m_l, N_HEADS * MLA_ROPE)
        kr_l = _axial_rope(proj[m_c:, COL_KROPE:COL_KROPE + MLA_ROPE].reshape(nb_l, l_l, MLA_ROPE))
        kr_l = kr_l.reshape(m_l, MLA_ROPE)
        tq = 256
        nq = l_l // tq
        rb_l = m_c // l_l
        y_mla_l = _mla_attention(
            (q, pl.BlockSpec((tq, GROUP_W), lambda b, i: (m_c // tq + b * nq + i, 0))),
            (qr_l, pl.BlockSpec((tq, N_HEADS * MLA_ROPE), lambda b, i: (b * nq + i, 0))),
            [((kv_cache, pl.BlockSpec((past, GROUP_W), lambda b, i: (b, 0))),
              (cache_mla_krope, pl.BlockSpec((None, None, past, MLA_ROPE), lambda b, i: (b, layer, 0, 0))),
              (kv_cache, pl.BlockSpec((past, GROUP_W), lambda b, i: (b, 1)))),
             ((kv, pl.BlockSpec((l_l, GROUP_W), lambda b, i: (rb_l + b, 0))),
              (kr_l, pl.BlockSpec((l_l, MLA_ROPE), lambda b, i: (b, 0))),
              (kv, pl.BlockSpec((l_l, GROUP_W), lambda b, i: (rb_l + b, 1))))],
            nb=nb_l, lq=l_l, tq=tq, name="mla_latent")

        lg = lg_all[layer]
        y_ret_c, s_c = _retention(proj, 0, nb_c, l_c, l_c, lg, ret_gn[layer], emit_state=True)
        y_ret_l = _retention(proj, m_c, nb_l, l_l, 256, lg, ret_gn[layer], state=state_ret, layer=layer)
        new_s.append(s_c)

        yp = _out_matmul((y_hy_c, y_na_c, y_mla_c, y_ret_c), (y_hy_l, y_na_l, y_mla_l, y_ret_l), w_out_b[layer])
        x1, h_ffn, top_i, top_g, pos, cnt = _post_router(
            yp, x, norm_mix_post[layer], ga, norm_ffn_pre[layer], cf, sf, rw_hi[layer], rw_lo[layer],
            rb_p[layer:layer + 1], grp_fn(256))

        counts = cnt[0, :n_exp].astype(jnp.int32)
        y_rows, dest = _moe_experts(h_ffn, top_i[:, :TOP_K], pos[:, :TOP_K], counts,
                                    moe_w1, moe_b1, moe_w2, moe_b2, layer)
        dest = dest.reshape(m_c + m_l, TOP_K)
        ys = [y_rows.at[dest[:, kk]].get(mode="promise_in_bounds") for kk in range(TOP_K)]
        x = _combine(ys, top_g, x1, norm_ffn_post[layer], gf, grp_fn(128))

    return (x[:m_c].reshape(nb_c, l_c, d), x[m_c:].reshape(nb_l, l_l, d),
            jnp.stack(new_k, axis=1), jnp.stack(new_v, axis=1), jnp.stack(new_ckv, axis=1),
            jnp.stack(new_kr, axis=1), jnp.stack(new_s, axis=1))
```

```python
import functools
import math

import jax
import jax.numpy as jnp
from jax import lax
from jax.experimental import pallas as pl
from jax.experimental.pallas import tpu as pltpu

F32 = jnp.float32
BF16 = jnp.bfloat16
SDS = jax.ShapeDtypeStruct

EPS = 1e-6
GRID_W = 64
HEAD_DIM = 128
N_HEADS = 8
GROUP_W = N_HEADS * HEAD_DIM
HY_BANDS = 16
HY_TARGET = 1e-2
HY_FAST_DECAY = 0.3
HY_SLOW_DECAY = 1.5
NA_WR = 8
NA_WC = 16
MLA_Q_LORA = 768
MLA_KV_LORA = 512
MLA_ROPE = 64
MLA_NOPE = 128
MLA_SCALE = (MLA_NOPE + MLA_ROPE) ** -0.5
ROPE_BASE = 10000.0
TOP_K = 4
SWIGLU_LIMIT = 7.0
SWIGLU_ALPHA = 1.702
LANES = 128

COL_HY = 0
COL_NA = 3072
COL_RET = 6144
COL_CKV = 10240
COL_QC = 10752
COL_KROPE = 11520
N_PROJ = 11776

VMEM_LIMIT_MB = 56


def _cp(sem, vmem_mb=48):
    return pltpu.CompilerParams(dimension_semantics=sem, vmem_limit_bytes=min(vmem_mb, VMEM_LIMIT_MB) * 2**20)


def _dot(a, b):
    return jnp.dot(a, b, preferred_element_type=F32)


def _dot_nt(a, b):
    return lax.dot_general(a, b, (((1,), (1,)), ((), ())), preferred_element_type=F32)


def _dot_tn(a, b):
    return lax.dot_general(a, b, (((0,), (0,)), ((), ())), preferred_element_type=F32)


def _silu(x):
    return x * jax.nn.sigmoid(x)


def _mod_body(c_ref, w_ref, b_ref, o_ref):
    s = _silu(c_ref[...])
    o_ref[...] = _dot(s.astype(BF16), w_ref[...].astype(BF16)) + b_ref[...]


def _modulation(cv, w_mod, b_mod, layer):
    _, d, n = w_mod.shape
    tn = 512
    return pl.pallas_call(
        _mod_body,
        grid=(n // tn,),
        in_specs=[pl.BlockSpec((8, d), lambda j: (0, 0)),
                  pl.BlockSpec((None, d, tn), lambda j: (layer, 0, j)),
                  pl.BlockSpec((None, 1, tn), lambda j: (layer, 0, j))],
        out_specs=pl.BlockSpec((8, tn), lambda j: (0, j)),
        out_shape=SDS((8, n), F32),
        compiler_params=_cp(("arbitrary",), 40),
        name="modulation",
    )(cv, w_mod, b_mod.reshape(b_mod.shape[0], 1, n))


def _norm_matmul_body(*refs, do_norm, has_mod, emit_h):
    it = iter(refs)
    x_ref = next(it)
    g_ref = next(it) if do_norm else None
    sc_ref = next(it) if has_mod else None
    sh_ref = next(it) if has_mod else None
    w_ref = next(it)
    o_ref = next(it)
    h_ref = next(it) if emit_h else None
    hs_ref = next(it)

    @pl.when(pl.program_id(1) == 0)
    def _():
        h = x_ref[...].astype(F32)
        if do_norm:
            h = h * lax.rsqrt(jnp.mean(h * h, axis=-1, keepdims=True) + EPS) * g_ref[...]
        if has_mod:
            h = h * (1.0 + sc_ref[...]) + sh_ref[...]
        if emit_h:
            h_ref[...] = h
        hs_ref[...] = h.astype(BF16)

    o_ref[...] = _dot(hs_ref[...], w_ref[...].astype(BF16)).astype(o_ref.dtype)


def _norm_matmul(x, xcol, k, w, *, tm, tn, gain=None, mod=None, grp=None, emit_h=False,
                 out_dtype=F32, xrow0=0, m=None, name="norm_matmul"):
    m = x.shape[0] if m is None else m
    n = w.shape[1]
    do_norm = gain is not None
    has_mod = mod is not None
    in_specs = [pl.BlockSpec((tm, k), lambda i, j: (i + xrow0, xcol))]
    args = [x]
    if do_norm:
        in_specs.append(pl.BlockSpec((1, k), lambda i, j: (0, 0)))
        args.append(gain.reshape(1, k))
    if has_mod:
        for a in mod:
            in_specs.append(pl.BlockSpec((None, 1, k), lambda i, j: (grp(i), 0, 0)))
            args.append(a)
    in_specs.append(pl.BlockSpec((k, tn), lambda i, j: (0, j)))
    args.append(w)
    out_specs = [pl.BlockSpec((tm, tn), lambda i, j: (i, j))]
    out_shape = [SDS((m, n), out_dtype)]
    if emit_h:
        out_specs.append(pl.BlockSpec((tm, k), lambda i, j: (i, 0)))
        out_shape.append(SDS((m, k), F32))
    res = pl.pallas_call(
        functools.partial(_norm_matmul_body, do_norm=do_norm, has_mod=has_mod, emit_h=emit_h),
        grid=(m // tm, n // tn),
        in_specs=in_specs, out_specs=out_specs, out_shape=out_shape,
        scratch_shapes=[pltpu.VMEM((tm, k), BF16)],
        compiler_params=_cp(("arbitrary", "arbitrary"), 48),
        name=name,
    )(*args)
    return res if emit_h else res[0]


def _dft_tables(l):
    n = 2 * l
    f = jnp.arange(l, dtype=jnp.int32)
    ang = ((f[:, None] * f[None, :]) % n).astype(F32) * (2.0 * math.pi / n)
    fc = jnp.cos(ang)
    alt = jnp.where(f % 2 == 0, 1.0, -1.0).astype(F32)
    fs = jnp.where(f[:, None] == 0, alt[None, :], jnp.sin(ang))
    return fc.astype(BF16), fs.astype(BF16), fs.T.astype(BF16)


def _hy_spec_body(fc_ref, fs_ref, hf_ref, hb_ref, a_ref, b_ref, *, l):
    hf = hf_ref[...]
    row = lax.broadcasted_iota(jnp.int32, hf.shape, 0)
    hb = jnp.where(row == 0, 0.0, hb_ref[...])
    a = _dot(fc_ref[...], (hf + hb).astype(BF16))
    p1 = _dot(fs_ref[...], hf.astype(BF16))
    p2 = _dot(fs_ref[...], hb.astype(BF16))
    orow = lax.broadcasted_iota(jnp.int32, a.shape, 0)
    first = jnp.logical_and(pl.program_id(0) == 0, orow == 0)
    wgt = jnp.where(first, 0.5 / l, 1.0 / l)
    a_ref[...] = wgt * a
    b_ref[...] = wgt * jnp.where(first, p1 + p2, p1 - p2)


def _hy_spectra(fc, fs, hf, hb):
    l, c2 = hf.shape
    tf = min(l, 256)
    tc = 512
    return pl.pallas_call(
        functools.partial(_hy_spec_body, l=l),
        grid=(l // tf, c2 // tc),
        in_specs=[pl.BlockSpec((tf, l), lambda i, j: (i, 0)),
                  pl.BlockSpec((tf, l), lambda i, j: (i, 0)),
                  pl.BlockSpec((l, tc), lambda i, j: (0, j)),
                  pl.BlockSpec((l, tc), lambda i, j: (0, j))],
        out_specs=[pl.BlockSpec((tf, tc), lambda i, j: (i, j))] * 2,
        out_shape=[SDS((l, c2), F32)] * 2,
        compiler_params=_cp(("arbitrary", "arbitrary"), 40),
        name="hyena_filter_spectra",
    )(fc, fs, hf, hb)


def _short_conv(x, w, b):
    l = x.shape[0]
    row = lax.broadcasted_iota(jnp.int32, x.shape, 0)
    xm = jnp.where(row == 0, 0.0, pltpu.roll(x, 1, 0))
    xp = jnp.where(row == l - 1, 0.0, pltpu.roll(x, l - 1, 0))
    return xm * w[0:1, :] + x * w[1:2, :] + xp * w[2:3, :] + b


def _hy_conv_body(*refs, first_order, nf):
    it = iter(refs)
    z_ref = next(it)
    zw_ref = next(it) if first_order else None
    zb_ref = next(it) if first_order else None
    gt_ref, gw_ref, gb_ref, skip_ref = next(it), next(it), next(it), next(it)
    ah_ref, bh_ref, fc_ref, fs_ref, gc_ref, gs_ref = (next(it) for _ in range(6))
    o_ref = next(it)
    zf_scr, zb_scr, acc = next(it), next(it), next(it)
    fb = pl.program_id(2)

    @pl.when(fb == 0)
    def _():
        z = z_ref[...].astype(F32)
        if first_order:
            z = _short_conv(z, zw_ref[...], zb_ref[...])
        zf_scr[...] = z
        zb_scr[...] = z.astype(BF16)
        acc[...] = jnp.zeros_like(acc)

    zb = zb_scr[...]
    uc = _dot(fc_ref[...], zb)
    us = _dot(fs_ref[...], zb)
    ah = ah_ref[...]
    bh = bh_ref[...]
    row = lax.broadcasted_iota(jnp.int32, uc.shape, 0)
    first = jnp.logical_and(fb == 0, row == 0)
    usb = us * bh
    yc = uc * ah - jnp.where(first, 0.0, usb)
    ys = jnp.where(first, usb, uc * bh + us * ah)
    acc[...] += _dot(gc_ref[...], yc.astype(BF16)) + _dot(gs_ref[...], ys.astype(BF16))

    @pl.when(fb == nf - 1)
    def _():
        gate = _short_conv(gt_ref[...].astype(F32), gw_ref[...], gb_ref[...])
        o_ref[...] = (gate * (acc[...] + zf_scr[...] * skip_ref[...])).astype(o_ref.dtype)


def _hy_conv(z, gate_col, proj, rb, conv_w, conv_b, skip, ah, bh, spec_col0, tables, *,
             nb, l, z_col, out_dtype):
    fc, fs, gs = tables
    first_order = z_col is not None
    tc = 512 if l > 512 else GROUP_W
    tf = min(l, 256)
    nf = l // tf

    def proj_specs(col):
        return [pl.BlockSpec((l, tc), lambda b, c, f: (b + rb, col // tc + c)),
                pl.BlockSpec((3, tc), lambda b, c, f: (0, (col - COL_HY) // tc + c)),
                pl.BlockSpec((1, tc), lambda b, c, f: (0, (col - COL_HY) // tc + c))]

    if first_order:
        in_specs = proj_specs(z_col)
        args = [proj, conv_w, conv_b]
    else:
        in_specs = [pl.BlockSpec((l, tc), lambda b, c, f: (b, c))]
        args = [z]
    in_specs += proj_specs(gate_col)
    args += [proj, conv_w, conv_b]
    in_specs += [pl.BlockSpec((1, tc), lambda b, c, f: (0, c)),
                 pl.BlockSpec((tf, tc), lambda b, c, f: (f, spec_col0 // tc + c)),
                 pl.BlockSpec((tf, tc), lambda b, c, f: (f, spec_col0 // tc + c)),
                 pl.BlockSpec((tf, l), lambda b, c, f: (f, 0)),
                 pl.BlockSpec((tf, l), lambda b, c, f: (f, 0)),
                 pl.BlockSpec((l, tf), lambda b, c, f: (0, f)),
                 pl.BlockSpec((l, tf), lambda b, c, f: (0, f))]
    args += [skip, ah, bh, fc, fs, fc, gs]
    return pl.pallas_call(
        functools.partial(_hy_conv_body, first_order=first_order, nf=nf),
        grid=(nb, GROUP_W // tc, nf),
        in_specs=in_specs,
        out_specs=pl.BlockSpec((l, tc), lambda b, c, f: (b, c)),
        out_shape=SDS((nb * l, GROUP_W), out_dtype),
        scratch_shapes=[pltpu.VMEM((l, tc), F32), pltpu.VMEM((l, tc), BF16), pltpu.VMEM((l, tc), F32)],
        compiler_params=_cp(("arbitrary", "arbitrary", "arbitrary"), 52),
        name="hyena_conv",
    )(*args)


def _hyena_filters(l, w1, b1, w2, b2, w3):
    hp = lax.Precision.HIGHEST
    pos = jnp.arange(l, dtype=F32)
    t = pos / l
    bands = jnp.linspace(1e-4, HY_BANDS - 1, HY_BANDS, dtype=F32)
    ang = (2.0 * math.pi * t)[:, None] * bands[None, :]
    z = jnp.concatenate([t[:, None], jnp.cos(ang), -jnp.sin(ang)], axis=-1)
    h = jnp.sin(jnp.dot(z, w1, precision=hp) + b1)
    h = jnp.sin(jnp.dot(h, w2, precision=hp) + b2)
    h = jnp.dot(h, w3, precision=hp)
    deltas = jnp.abs(jnp.linspace(math.log(HY_TARGET) / HY_FAST_DECAY,
                                  math.log(HY_TARGET) / HY_SLOW_DECAY, GROUP_W, dtype=F32))
    window = jnp.exp(-t[:, None] * deltas[None, :])
    filt = h.reshape(l, 2, 2, GROUP_W) * window[:, None, None, :]
    return filt[:, :, 0, :].reshape(l, 2 * GROUP_W), filt[:, :, 1, :].reshape(l, 2 * GROUP_W)


def _hyena(proj, row0, nb, l, lw, tables):
    fc, fs, gs = tables
    hf, hb = _hyena_filters(l, lw["hy_filt_w1"], lw["hy_filt_b1"], lw["hy_filt_w2"], lw["hy_filt_b2"],
                            lw["hy_filt_w3"])
    ah, bh = _hy_spectra(fc, fs, hf, hb)
    cw = lw["hy_conv_w"]
    cb = lw["hy_conv_b"].reshape(1, -1)
    skip = lw["hy_bias"]
    rb = row0 // l
    z1 = _hy_conv(None, COL_HY, proj, rb, cw, cb, skip[0:1], ah, bh, 0, tables,
                  nb=nb, l=l, z_col=COL_HY + 2 * GROUP_W, out_dtype=F32)
    return _hy_conv(z1, COL_HY + GROUP_W, proj, rb, cw, cb, skip[1:2], ah, bh, GROUP_W, tables,
                    nb=nb, l=l, z_col=None, out_dtype=BF16)


def _softmax_attend(scores, values):
    m = functools.reduce(jnp.maximum, [jnp.max(s, axis=-1, keepdims=True) for s in scores])
    ps = [jnp.exp(s - m) for s in scores]
    den = functools.reduce(jnp.add, [jnp.sum(p, axis=-1, keepdims=True) for p in ps])
    o = functools.reduce(jnp.add, [_dot(p.astype(BF16), v) for p, v in zip(ps, values)])
    return o / den


def _na_ctx_body(q_ref, k_ref, v_ref, o_ref, kc_ref, vc_ref):
    scale = HEAD_DIM ** -0.5
    for h in range(N_HEADS):
        hs = slice(h * HEAD_DIM, (h + 1) * HEAD_DIM)
        k = k_ref[:, hs]
        v = v_ref[:, hs]
        kc_ref[h] = k
        vc_ref[h] = v
        s = _dot_nt(q_ref[:, hs].astype(BF16), k.astype(BF16)) * scale
        o_ref[:, hs] = _softmax_attend([s], [v.astype(BF16)]).astype(o_ref.dtype)


def _na_context(proj, nb, l):
    c0 = COL_NA // GROUP_W
    return pl.pallas_call(
        _na_ctx_body,
        grid=(nb,),
        in_specs=[pl.BlockSpec((l, GROUP_W), lambda b: (b, c0)),
                  pl.BlockSpec((l, GROUP_W), lambda b: (b, c0 + 1)),
                  pl.BlockSpec((l, GROUP_W), lambda b: (b, c0 + 2))],
        out_specs=[pl.BlockSpec((l, GROUP_W), lambda b: (b, 0)),
                   pl.BlockSpec((None, N_HEADS, l, HEAD_DIM), lambda b: (b, 0, 0, 0)),
                   pl.BlockSpec((None, N_HEADS, l, HEAD_DIM), lambda b: (b, 0, 0, 0))],
        out_shape=[SDS((nb * l, GROUP_W), BF16),
                   SDS((nb, N_HEADS, l, HEAD_DIM), F32),
                   SDS((nb, N_HEADS, l, HEAD_DIM), F32)],
        compiler_params=_cp(("arbitrary",), 32),
        name="na_context",
    )(proj, proj, proj)


def _na_bias_table(rpb):
    cidx = jnp.arange(GRID_W)
    col_start = jnp.clip(cidx - NA_WC // 2, 0, GRID_W - NA_WC)
    col_in = (cidx[None, :] >= col_start[:, None]) & (cidx[None, :] < col_start[:, None] + NA_WC)
    col_off = jnp.clip(cidx[None, :] - cidx[:, None], -(NA_WC - 1), NA_WC - 1) + (NA_WC - 1)
    t = rpb[:, :, col_off].astype(F32)
    t = jnp.where(col_in[None, None], t, -jnp.inf)
    return t.transpose(0, 2, 1, 3).reshape(rpb.shape[0], GRID_W, -1)


def _na_lat_body(q_ref, k_ref, v_ref, kc_ref, vc_ref, bt_ref, o_ref, *, rows):
    scale = HEAD_DIM ** -0.5
    wr = min(NA_WR, rows)
    kcx = kc_ref[...].astype(BF16)
    vcx = vc_ref[...].astype(BF16)
    nloc = wr * GRID_W
    bt = bt_ref[...]
    for r in range(rows):
        rs = min(max(r - wr // 2, 0), rows - wr)
        off = rs - r + (NA_WR - 1)
        q = q_ref[r * GRID_W:(r + 1) * GRID_W, :].astype(BF16)
        kl = k_ref[rs * GRID_W:rs * GRID_W + nloc, :].astype(BF16)
        vl = v_ref[rs * GRID_W:rs * GRID_W + nloc, :].astype(BF16)
        s_loc = _dot_nt(q, kl) * scale + bt[:, off * GRID_W:off * GRID_W + nloc]
        s_ctx = _dot_nt(q, kcx) * scale
        o_ref[r * GRID_W:(r + 1) * GRID_W, :] = _softmax_attend([s_loc, s_ctx], [vl, vcx]).astype(o_ref.dtype)


def _na_latent(proj, row0, nb, l, cache_k, cache_v, layer, bias_table):
    rows = l // GRID_W
    rb = row0 // l
    c0 = COL_NA // HEAD_DIM
    past = cache_k.shape[3]
    cspec = pl.BlockSpec((None, None, None, past, HEAD_DIM), lambda b, h: (b, layer, h, 0, 0))
    return pl.pallas_call(
        functools.partial(_na_lat_body, rows=rows),
        grid=(nb, N_HEADS),
        in_specs=[pl.BlockSpec((l, HEAD_DIM), lambda b, h: (b + rb, c0 + h)),
                  pl.BlockSpec((l, HEAD_DIM), lambda b, h: (b + rb, c0 + N_HEADS + h)),
                  pl.BlockSpec((l, HEAD_DIM), lambda b, h: (b + rb, c0 + 2 * N_HEADS + h)),
                  cspec, cspec,
                  pl.BlockSpec((None, GRID_W, bias_table.shape[2]), lambda b, h: (h, 0, 0))],
        out_specs=pl.BlockSpec((l, HEAD_DIM), lambda b, h: (b, h)),
        out_shape=SDS((nb * l, GROUP_W), BF16),
        compiler_params=_cp(("arbitrary", "arbitrary"), 32),
        name="na_latent",
    )(proj, proj, proj, cache_k, cache_v, bias_table)


def _mla_attn_body(*refs, nseg):
    qn_ref, qr_ref = refs[0], refs[1]
    segs = [refs[2 + 3 * i: 5 + 3 * i] for i in range(nseg)]
    o_ref = refs[2 + 3 * nseg]
    krs = [kr_ref[...][:, :MLA_ROPE].astype(BF16) for _, kr_ref, _ in segs]
    qr_all = qr_ref[...].astype(BF16)
    for h in range(N_HEADS):
        hs = slice(h * HEAD_DIM, (h + 1) * HEAD_DIM)
        qn = qn_ref[:, hs].astype(BF16)
        qr = qr_all[:, h * MLA_ROPE:(h + 1) * MLA_ROPE]
        scores = [(_dot_nt(qn, kn_ref[:, hs].astype(BF16)) + _dot_nt(qr, kr)) * MLA_SCALE
                  for (kn_ref, _, _), kr in zip(segs, krs)]
        vals = [v_ref[:, hs].astype(BF16) for _, _, v_ref in segs]
        o_ref[:, hs] = _softmax_attend(scores, vals).astype(o_ref.dtype)


def _mla_attention(qn, qr, segs, *, nb, lq, tq, name):
    arrays = [qn[0], qr[0]]
    specs = [qn[1], qr[1]]
    for s in segs:
        for a, sp in s:
            arrays.append(a)
            specs.append(sp)
    return pl.pallas_call(
        functools.partial(_mla_attn_body, nseg=len(segs)),
        grid=(nb, lq // tq),
        in_specs=specs,
        out_specs=pl.BlockSpec((tq, GROUP_W), lambda b, i: (b * (lq // tq) + i, 0)),
        out_shape=SDS((nb * lq, GROUP_W), BF16),
        compiler_params=_cp(("arbitrary", "arbitrary"), 48),
        name=name,
    )(*arrays)


def _axial_rope(x):
    l = x.shape[-2]
    pos = jnp.arange(l)
    half = MLA_ROPE // 2
    freqs = jnp.power(ROPE_BASE, -jnp.arange(0, half, 2, dtype=F32) / half)

    def rot(xa, p):
        ang = p.astype(F32)[:, None] * freqs[None, :]
        cos, sin = jnp.cos(ang), jnp.sin(ang)
        a1, a2 = jnp.split(xa, 2, axis=-1)
        return jnp.concatenate([a1 * cos - a2 * sin, a2 * cos + a1 * sin], axis=-1)

    return jnp.concatenate([rot(x[..., :half], pos // GRID_W), rot(x[..., half:], pos % GRID_W)], axis=-1)


def _ret_body(*refs, l, tq, has_state, emit_state):
    it = iter(refs)
    lg_ref = next(it)
    q_ref, k_ref, v_ref, g_ref, gn_ref = (next(it) for _ in range(5))
    s0_ref = next(it) if has_state else None
    o_ref = next(it)
    so_ref = next(it) if emit_state else None
    t0 = pl.program_id(1) * tq
    ti = lax.broadcasted_iota(jnp.int32, (tq, l), 0) + t0
    si = lax.broadcasted_iota(jnp.int32, (tq, l), 1)
    d = (ti - si).astype(F32)
    tcol = (lax.broadcasted_iota(jnp.int32, (tq, 1), 0) + t0).astype(F32)
    scol = lax.broadcasted_iota(jnp.int32, (l, 1), 0).astype(F32)
    kscale = HEAD_DIM ** -0.5
    for h in range(N_HEADS):
        hs = slice(h * HEAD_DIM, (h + 1) * HEAD_DIM)
        lgf = lg_ref[0, h]
        lgb = lg_ref[1, h]
        q = q_ref[:, hs]
        k = k_ref[:, hs] * kscale
        v = v_ref[:, hs].astype(BF16)
        s = _dot_nt(q.astype(BF16), k.astype(BF16))
        w = jnp.exp(jnp.where(d > 0, lgf, -lgb) * d)
        w = jnp.where(d == 0, 2.0, w)
        o = _dot((s * w).astype(BF16), v)
        if has_state:
            qf = q * jnp.exp(lgf * (tcol + 1.0))
            qb = q * jnp.exp(lgb * (l - tcol))
            o = o + _dot(qf.astype(BF16), s0_ref[0, h].astype(BF16)) + _dot(qb.astype(BF16), s0_ref[1, h].astype(BF16))
        o = o * lax.rsqrt(jnp.mean(o * o, axis=-1, keepdims=True) + EPS) * gn_ref[:, hs]
        o_ref[:, hs] = (o * _silu(g_ref[:, hs])).astype(o_ref.dtype)
        if emit_state:
            kf = k * jnp.exp(lgf * (l - 1.0 - scol))
            kb = k * jnp.exp(lgb * scol)
            so_ref[0, h] = _dot_tn(kf.astype(BF16), v)
            so_ref[1, h] = _dot_tn(kb.astype(BF16), v)


def _retention(proj, row0, nb, l, tq, lg, gn, state=None, layer=0, emit_state=False):
    assert not emit_state or tq == l
    nq = l // tq
    rbq = row0 // tq
    rbl = row0 // l
    c0 = COL_RET // GROUP_W
    in_specs = [pl.BlockSpec(memory_space=pltpu.SMEM),
                pl.BlockSpec((tq, GROUP_W), lambda b, i: (rbq + b * nq + i, c0)),
                pl.BlockSpec((l, GROUP_W), lambda b, i: (rbl + b, c0 + 1)),
                pl.BlockSpec((l, GROUP_W), lambda b, i: (rbl + b, c0 + 2)),
                pl.BlockSpec((tq, GROUP_W), lambda b, i: (rbq + b * nq + i, c0 + 3)),
                pl.BlockSpec((1, GROUP_W), lambda b, i: (0, 0))]
    args = [lg, proj, proj, proj, proj, gn.reshape(1, GROUP_W)]
    if state is not None:
        in_specs.append(pl.BlockSpec((None, None, 2, N_HEADS, HEAD_DIM, HEAD_DIM),
                                     lambda b, i: (b, layer, 0, 0, 0, 0)))
        args.append(state)
    out_specs = [pl.BlockSpec((tq, GROUP_W), lambda b, i: (b * nq + i, 0))]
    out_shape = [SDS((nb * l, GROUP_W), BF16)]
    if emit_state:
        out_specs.append(pl.BlockSpec((None, 2, N_HEADS, HEAD_DIM, HEAD_DIM), lambda b, i: (b, 0, 0, 0, 0)))
        out_shape.append(SDS((nb, 2, N_HEADS, HEAD_DIM, HEAD_DIM), F32))
    res = pl.pallas_call(
        functools.partial(_ret_body, l=l, tq=tq, has_state=state is not None, emit_state=emit_state),
        grid=(nb, nq),
        in_specs=in_specs, out_specs=out_specs, out_shape=out_shape,
        compiler_params=_cp(("arbitrary", "arbitrary"), 48),
        name="retention",
    )(*args)
    return res if emit_state else res[0]


def _out_matmul_body(*refs, nc):
    ctx, lat, w_ref, o_ref = refs[0:4], refs[4:8], refs[8], refs[9]

    def project(ys):
        acc = _dot(ys[0][...], w_ref[0 * GROUP_W:1 * GROUP_W, :])
        acc += _dot(ys[1][...], w_ref[1 * GROUP_W:2 * GROUP_W, :])
        acc += _dot(ys[2][...], w_ref[2 * GROUP_W:3 * GROUP_W, :])
        acc += _dot(ys[3][...], w_ref[3 * GROUP_W:4 * GROUP_W, :])
        o_ref[...] = acc

    @pl.when(pl.program_id(1) < nc)
    def _():
        project(ctx)

    @pl.when(pl.program_id(1) >= nc)
    def _():
        project(lat)


def _out_matmul(ys_ctx, ys_lat, w_out, *, tm=512, tn=1024):
    m_c = ys_ctx[0].shape[0]
    m = m_c + ys_lat[0].shape[0]
    nc = m_c // tm
    d = w_out.shape[1]
    cspec = pl.BlockSpec((tm, GROUP_W), lambda j, i: (jnp.minimum(i, nc - 1), 0))
    lspec = pl.BlockSpec((tm, GROUP_W), lambda j, i: (jnp.maximum(i - nc, 0), 0))
    return pl.pallas_call(
        functools.partial(_out_matmul_body, nc=nc),
        grid=(d // tn, m // tm),
        in_specs=[cspec] * 4 + [lspec] * 4 + [pl.BlockSpec((4 * GROUP_W, tn), lambda j, i: (0, j))],
        out_specs=pl.BlockSpec((tm, tn), lambda j, i: (i, j)),
        out_shape=SDS((m, d), F32),
        compiler_params=_cp(("arbitrary", "arbitrary"), 48),
        name="out_proj",
    )(*ys_ctx, *ys_lat, w_out)


def _topk(logits):
    lane = lax.broadcasted_iota(jnp.int32, logits.shape, 1)
    work = logits
    idx_out = jnp.zeros(logits.shape, jnp.int32)
    val_out = jnp.full(logits.shape, -jnp.inf, F32)
    chosen = jnp.zeros(logits.shape, F32)
    idxs = []
    for k in range(TOP_K):
        m = jnp.max(work, axis=-1, keepdims=True)
        idx = jnp.min(jnp.where(work == m, lane, LANES), axis=-1, keepdims=True)
        idx_out = jnp.where(lane == k, idx, idx_out)
        val_out = jnp.where(lane == k, m, val_out)
        chosen = jnp.where(lane == idx, 1.0, chosen)
        work = jnp.where(lane == idx, -jnp.inf, work)
        idxs.append(idx)
    e = jnp.exp(val_out - jnp.max(val_out, axis=-1, keepdims=True))
    return idx_out, e / jnp.sum(e, axis=-1, keepdims=True), chosen, idxs


def _post_router_body(yp_ref, x_ref, gpost_ref, ga_ref, gpre_ref, cf_ref, sf_ref, rwh_ref, rwl_ref, rb_ref,
                      x1_ref, h_ref, ti_ref, tg_ref, pos_ref, cnt_ref, carry):
    @pl.when(pl.program_id(0) == 0)
    def _():
        carry[...] = jnp.zeros_like(carry)

    y = yp_ref[...]
    y = y * lax.rsqrt(jnp.mean(y * y, axis=-1, keepdims=True) + EPS) * gpost_ref[...]
    x1 = x_ref[...] + ga_ref[...] * y
    x1_ref[...] = x1
    h = x1 * lax.rsqrt(jnp.mean(x1 * x1, axis=-1, keepdims=True) + EPS) * gpre_ref[...]
    h = h * (1.0 + cf_ref[...]) + sf_ref[...]
    h_ref[...] = h.astype(h_ref.dtype)
    h_hi = h.astype(BF16)
    h_lo = (h - h_hi.astype(F32)).astype(BF16)
    logits = _dot(h_hi, rwh_ref[...]) + _dot(h_lo, rwh_ref[...]) + _dot(h_hi, rwl_ref[...]) + rb_ref[...]
    idx_out, gates, chosen, idxs = _topk(logits)
    ti_ref[...] = idx_out
    tg_ref[...] = gates
    tm = chosen.shape[0]
    r = lax.broadcasted_iota(jnp.int32, (tm, tm), 0)
    c = lax.broadcasted_iota(jnp.int32, (tm, tm), 1)
    before = _dot(jnp.where(c < r, 1.0, 0.0).astype(BF16), chosen.astype(BF16)) + carry[0:1, :]
    lane = lax.broadcasted_iota(jnp.int32, chosen.shape, 1)
    pos = jnp.zeros(chosen.shape, jnp.int32)
    for k, idx in enumerate(idxs):
        pk = jnp.sum(jnp.where(lane == idx, before, 0.0), axis=-1, keepdims=True)
        pos = jnp.where(lane == k, pk.astype(jnp.int32), pos)
    pos_ref[...] = pos
    carry[...] = carry[...] + jnp.sum(chosen, axis=0, keepdims=True)
    cnt_ref[...] = carry[...]


def _post_router(yp, x, gpost, ga, gpre, cf, sf, rw_hi, rw_lo, rb, grp, *, tm=256):
    m, d = x.shape
    row = pl.BlockSpec((1, d), lambda i: (0, 0))
    modspec = pl.BlockSpec((None, 1, d), lambda i: (grp(i), 0, 0))
    full = pl.BlockSpec((tm, d), lambda i: (i, 0))
    small = pl.BlockSpec((tm, LANES), lambda i: (i, 0))
    rwspec = pl.BlockSpec((d, LANES), lambda i: (0, 0))
    return pl.pallas_call(
        _post_router_body,
        grid=(m // tm,),
        in_specs=[full, full, row, modspec, row, modspec, modspec, rwspec, rwspec,
                  pl.BlockSpec((1, LANES), lambda i: (0, 0))],
        out_specs=[full, full, small, small, small, pl.BlockSpec((8, LANES), lambda i: (0, 0))],
        out_shape=[SDS((m, d), F32), SDS((m, d), BF16), SDS((m, LANES), jnp.int32), SDS((m, LANES), F32),
                   SDS((m, LANES), jnp.int32), SDS((8, LANES), F32)],
        scratch_shapes=[pltpu.VMEM((8, LANES), F32)],
        compiler_params=_cp(("arbitrary",), 52),
        name="post_norm_router",
    )(yp, x, gpost.reshape(1, d), ga, gpre.reshape(1, d), cf, sf, rw_hi, rw_lo, rb)


def _deinterleave(hh):
    tm, tn = hh.shape
    lane = lax.broadcasted_iota(jnp.int32, (tm, LANES), 1)
    half = LANES // 2
    idx = jnp.where(lane < half, 2 * lane, 2 * (lane - half) + 1)
    ev, od = [], []
    for c in range(tn // (2 * LANES)):
        g0 = jnp.take_along_axis(hh[:, (2 * c) * LANES:(2 * c + 1) * LANES], idx, axis=1)
        g1 = jnp.take_along_axis(hh[:, (2 * c + 1) * LANES:(2 * c + 2) * LANES], idx, axis=1)
        ev.append(jnp.where(lane < half, g0, pltpu.roll(g1, half, 1)))
        od.append(jnp.where(lane < half, pltpu.roll(g0, half, 1), g1))
    return jnp.concatenate(ev, axis=1), jnp.concatenate(od, axis=1)


def _cast_weights(w_ref, wb_ref, chunk=256):
    k = w_ref.shape[0]

    def step(c, carry):
        r = pl.multiple_of(c * chunk, chunk)
        wb_ref[pl.ds(r, chunk), :] = w_ref[pl.ds(r, chunk), :].astype(BF16)
        return carry

    lax.fori_loop(0, k // chunk, step, 0)


ROW_BLOCK = 512
ROW_PAD = 256


def _swiglu(hh):
    glu, lin = _deinterleave(hh)
    glu = jnp.minimum(glu, SWIGLU_LIMIT)
    lin = jnp.clip(lin, -SWIGLU_LIMIT, SWIGLU_LIMIT)
    return glu * jax.nn.sigmoid(SWIGLU_ALPHA * glu) * (lin + 1.0)


def _expert_body(n5_ref, tl_ref, r0_ref, lf_ref, x_hbm, w_ref, b_ref, o_hbm, wb, xbuf, obuf, xtail, otail,
                 sem_in, sem_out, sem_tail, *, swiglu, tn_out):
    e = pl.program_id(0)
    n5 = n5_ref[e]
    tl = tl_ref[e]
    r0 = r0_ref[e]
    col0 = pl.multiple_of(pl.program_id(1) * tn_out, tn_out)

    def x_dma(b, slot):
        row = pl.multiple_of(r0 + b * ROW_BLOCK, ROW_PAD)
        return pltpu.make_async_copy(x_hbm.at[pl.ds(row, ROW_BLOCK), :], xbuf.at[slot], sem_in.at[slot])

    def o_dma(b, slot):
        row = pl.multiple_of(r0 + b * ROW_BLOCK, ROW_PAD)
        return pltpu.make_async_copy(obuf.at[slot], o_hbm.at[pl.ds(row, ROW_BLOCK), pl.ds(col0, tn_out)],
                                     sem_out.at[slot])

    def tail_in_dma(ee):
        row = pl.multiple_of(r0_ref[ee] + n5_ref[ee] * ROW_BLOCK, ROW_PAD)
        return pltpu.make_async_copy(x_hbm.at[pl.ds(row, ROW_PAD), :], xtail, sem_tail.at[0])

    def start_first_inputs(ee):
        @pl.when(n5_ref[ee] > 0)
        def _():
            row = pl.multiple_of(r0_ref[ee], ROW_PAD)
            pltpu.make_async_copy(x_hbm.at[pl.ds(row, ROW_BLOCK), :], xbuf.at[0], sem_in.at[0]).start()

        @pl.when(tl_ref[ee] == 1)
        def _():
            tail_in_dma(ee).start()

    trow = pl.multiple_of(r0 + n5 * ROW_BLOCK, ROW_PAD)
    tail_out = pltpu.make_async_copy(otail, o_hbm.at[pl.ds(trow, ROW_PAD), pl.ds(col0, tn_out)], sem_tail.at[1])

    def compute(x):
        hh = _dot(x, wb[...]) + b_ref[...]
        return _swiglu(hh) if swiglu else hh

    @pl.when(jnp.logical_and(e == 0, pl.program_id(1) == 0))
    def _():
        start_first_inputs(0)

    @pl.when(n5 + tl > 0)
    def _():
        _cast_weights(w_ref, wb)

        def loop(b, carry):
            slot = lax.rem(b, 2)
            x_dma(b, slot).wait()

            @pl.when(b + 1 < n5)
            def _():
                x_dma(b + 1, 1 - slot).start()

            @pl.when(b >= 2)
            def _():
                o_dma(b - 2, slot).wait()

            obuf[slot] = compute(xbuf[slot]).astype(obuf.dtype)
            o_dma(b, slot).start()
            return carry

        lax.fori_loop(0, n5, loop, 0)

        @pl.when(n5 >= 2)
        def _():
            o_dma(n5 - 2, lax.rem(n5, 2)).wait()

        @pl.when(n5 >= 1)
        def _():
            o_dma(n5 - 1, lax.rem(n5 + 1, 2)).wait()

        @pl.when(tl == 1)
        def _():
            tail_in_dma(e).wait()
            otail[...] = compute(xtail[...]).astype(otail.dtype)
            tail_out.start()
            tail_out.wait()

    last_tile = pl.program_id(1) == pl.num_programs(1) - 1
    nxt = jnp.where(last_tile, e + 1, e)

    @pl.when(nxt < pl.num_programs(0))
    def _():
        start_first_inputs(nxt)

    @pl.when(e == pl.num_programs(0) - 1)
    def _():
        otail[...] = jnp.zeros_like(otail)

        def z_dma(c):
            row = pl.multiple_of(lf_ref[0] + c * ROW_PAD, ROW_PAD)
            return pltpu.make_async_copy(otail, o_hbm.at[pl.ds(row, ROW_PAD), pl.ds(col0, tn_out)], sem_tail.at[1])

        def start(c, carry):
            z_dma(c).start()
            return carry

        def wait(c, carry):
            z_dma(c).wait()
            return carry

        lax.fori_loop(0, lf_ref[1], start, 0)
        lax.fori_loop(0, lf_ref[1], wait, 0)


def _expert_matmul(sched, x, w, b, layer, *, tn, swiglu, out_dtype, vmem_mb, name):
    n_rows, k = x.shape
    n_exp, _, n = w.shape[1:]
    tn_out = tn // 2 if swiglu else tn
    n_out = n // 2 if swiglu else n
    return pl.pallas_call(
        functools.partial(_expert_body, swiglu=swiglu, tn_out=tn_out),
        grid_spec=pltpu.PrefetchScalarGridSpec(
            num_scalar_prefetch=4,
            grid=(n_exp, n // tn),
            in_specs=[pl.BlockSpec(memory_space=pl.ANY),
                      pl.BlockSpec((None, None, k, tn), lambda e, j, n5, tl, r0, lf: (layer, e, 0, j)),
                      pl.BlockSpec((None, None, 1, tn), lambda e, j, n5, tl, r0, lf: (layer, e, 0, j))],
            out_specs=pl.BlockSpec(memory_space=pl.ANY),
            scratch_shapes=[pltpu.VMEM((k, tn), BF16),
                            pltpu.VMEM((2, ROW_BLOCK, k), x.dtype),
                            pltpu.VMEM((2, ROW_BLOCK, tn_out), out_dtype),
                            pltpu.VMEM((ROW_PAD, k), x.dtype),
                            pltpu.VMEM((ROW_PAD, tn_out), out_dtype),
                            pltpu.SemaphoreType.DMA((2,)),
                            pltpu.SemaphoreType.DMA((2,)),
                            pltpu.SemaphoreType.DMA((2,))]),
        out_shape=SDS((n_rows, n_out), out_dtype),
        compiler_params=_cp(("arbitrary", "arbitrary"), vmem_mb),
        name=name,
    )(*sched, x, w, b.reshape(b.shape[0], n_exp, 1, n))


def _moe_experts(h, top_idx, pos, counts, w1, b1, w2, b2, layer, *, tn1=1024, tn2=1024):
    m, d = h.shape
    n_exp = w1.shape[1]
    n_assign = m * TOP_K
    flat_e = top_idx.reshape(-1)
    padded = (counts + ROW_PAD - 1) // ROW_PAD * ROW_PAD
    row0 = (jnp.cumsum(padded) - padded).astype(jnp.int32)
    dest = (row0[flat_e] + pos.reshape(-1)).astype(jnp.int32)
    n_rows = n_assign + n_exp * ROW_PAD
    tok = jnp.arange(n_assign, dtype=jnp.int32) // TOP_K
    row_tok = jnp.zeros((n_rows,), jnp.int32).at[dest].set(tok, mode="promise_in_bounds", unique_indices=True)
    xs = h.at[row_tok].get(mode="promise_in_bounds")
    total = jnp.sum(padded).astype(jnp.int32)
    leftover = jnp.stack([total, (n_rows - total) // ROW_PAD])
    sched = ((padded // ROW_BLOCK).astype(jnp.int32), ((padded // ROW_PAD) % 2).astype(jnp.int32), row0, leftover)
    act = _expert_matmul(sched, xs, w1, b1, layer, tn=tn1, swiglu=True, out_dtype=BF16, vmem_mb=56,
                         name="moe_w1")
    y_rows = _expert_matmul(sched, act, w2, b2, layer, tn=tn2, swiglu=False, out_dtype=BF16, vmem_mb=48,
                            name="moe_w2")
    return y_rows, dest


def _combine_body(y0_ref, y1_ref, y2_ref, y3_ref, tg_ref, x_ref, g_ref, gf_ref, o_ref):
    tg = tg_ref[...]
    y0, y1, y2, y3 = (r[...].astype(F32) for r in (y0_ref, y1_ref, y2_ref, y3_ref))
    y = (y0 * tg[:, 0:1] + y1 * tg[:, 1:2]) + (y2 * tg[:, 2:3] + y3 * tg[:, 3:4])
    inv = lax.rsqrt(jnp.mean(y * y, axis=-1, keepdims=True) + EPS)
    o_ref[...] = x_ref[...] + gf_ref[...] * (y * inv * g_ref[...])


def _combine(ys, top_gate, x, gpost, gf, grp, *, tm=128):
    m, d = x.shape
    full = pl.BlockSpec((tm, d), lambda i: (i, 0))
    return pl.pallas_call(
        _combine_body,
        grid=(m // tm,),
        in_specs=[full, full, full, full, pl.BlockSpec((tm, LANES), lambda i: (i, 0)), full,
                  pl.BlockSpec((1, d), lambda i: (0, 0)),
                  pl.BlockSpec((None, 1, d), lambda i: (grp(i), 0, 0))],
        out_specs=full,
        out_shape=SDS((m, d), F32),
        compiler_params=_cp(("arbitrary",), 56),
        name="moe_combine",
    )(*ys, top_gate, x, gpost.reshape(1, d), gf)


def _prep_w_in(w_in):
    hy_na = 6 * GROUP_W
    mla0 = hy_na
    ckv0 = mla0 + MLA_Q_LORA
    kr0 = ckv0 + MLA_KV_LORA
    ret0 = kr0 + MLA_ROPE
    pad = jnp.zeros(w_in.shape[:2] + (N_PROJ - COL_KROPE - MLA_ROPE,), w_in.dtype)
    parts = [w_in[..., :hy_na], w_in[..., ret0:], w_in[..., ckv0:kr0], w_in[..., mla0:ckv0],
             w_in[..., kr0:ret0], pad]
    return jnp.concatenate(parts, axis=-1).astype(BF16)


def _prep_heads(w, d_a, d_b):
    lead = w.shape[:-1]
    w = w.reshape(lead + (N_HEADS, d_a + d_b))
    a = w[..., :d_a].reshape(lead + (N_HEADS * d_a,))
    b = w[..., d_a:].reshape(lead + (N_HEADS * d_b,))
    return jnp.concatenate([a, b], axis=-1).astype(BF16)


def kernel(x_prompt, x_sample, cache_na_k, cache_na_v, cache_mla_ckv, cache_mla_krope, state_ret, c, c_ctx, w_mod, b_mod, norm_mix_pre, norm_mix_post, norm_ffn_pre, norm_ffn_post, w_in, hy_conv_w, hy_conv_b, hy_filt_w1, hy_filt_b1, hy_filt_w2, hy_filt_b2, hy_filt_w3, hy_bias, na_rpb, mla_q_norm, mla_w_uq, mla_kv_norm, mla_w_ukv, ret_decay, ret_gn, w_out, router_w, router_b, moe_w1, moe_b1, moe_w2, moe_b2):
    nb_c, l_c, d = x_prompt.shape
    nb_l, l_l, _ = x_sample.shape
    depth = w_in.shape[0]
    past = cache_na_k.shape[3]
    m_c = nb_c * l_c
    m_l = nb_l * l_l
    n_exp = router_w.shape[2]

    x = jnp.concatenate([x_prompt.reshape(m_c, d), x_sample.reshape(m_l, d)], axis=0)
    cv = jnp.zeros((8, d), F32).at[0].set(c_ctx).at[1:1 + nb_l].set(c)

    def grp_fn(tm):
        nc = m_c // tm
        per = l_l // tm
        return lambda i: jnp.where(i < nc, 0, 1 + (i - nc) // per)

    w_in_p = _prep_w_in(w_in)
    w_out_b = w_out.astype(BF16)
    w_uq_p = _prep_heads(mla_w_uq, MLA_NOPE, MLA_ROPE)
    w_ukv_p = _prep_heads(mla_w_ukv, MLA_NOPE, HEAD_DIM)
    rw_p = jnp.pad(router_w, ((0, 0), (0, 0), (0, LANES - n_exp)))
    rw_hi = rw_p.astype(BF16)
    rw_lo = (rw_p - rw_hi.astype(F32)).astype(BF16)
    rb_p = jnp.pad(router_b, ((0, 0), (0, LANES - n_exp)), constant_values=-jnp.inf)
    tables_c = _dft_tables(l_c)
    tables_l = _dft_tables(l_l)
    lg_all = jax.nn.log_sigmoid(ret_decay.astype(F32))
    ckv_cache_rows = cache_mla_ckv.reshape(nb_l * depth * past, MLA_KV_LORA)

    new_k, new_v, new_ckv, new_kr, new_s = [], [], [], [], []
    for layer in range(depth):
        lw = {"hy_conv_w": hy_conv_w[layer], "hy_conv_b": hy_conv_b[layer],
              "hy_filt_w1": hy_filt_w1[layer], "hy_filt_b1": hy_filt_b1[layer],
              "hy_filt_w2": hy_filt_w2[layer], "hy_filt_b2": hy_filt_b2[layer],
              "hy_filt_w3": hy_filt_w3[layer], "hy_bias": hy_bias[layer]}
        mod = _modulation(cv, w_mod, b_mod, layer)
        sa, ca, ga, sf, cf, gf = [a.reshape(8, 1, d) for a in jnp.split(mod, 6, axis=-1)]

        proj = _norm_matmul(x, 0, d, w_in_p[layer], tm=512, tn=512, gain=norm_mix_pre[layer],
                            mod=(ca, sa), grp=grp_fn(512), name="proj_in")

        y_hy_c = _hyena(proj, 0, nb_c, l_c, lw, tables_c)
        y_hy_l = _hyena(proj, m_c, nb_l, l_l, lw, tables_l)

        y_na_c, k_c, v_c = _na_context(proj, nb_c, l_c)
        y_na_l = _na_latent(proj, m_c, nb_l, l_l, cache_na_k, cache_na_v, layer, _na_bias_table(na_rpb[layer]))
        new_k.append(k_c)
        new_v.append(v_c)

        q = _norm_matmul(proj, COL_QC // MLA_Q_LORA, MLA_Q_LORA, w_uq_p[layer], tm=512, tn=512,
                         gain=mla_q_norm[layer], name="mla_q")
        kv, ckv_n = _norm_matmul(proj, COL_CKV // MLA_KV_LORA, MLA_KV_LORA, w_ukv_p[layer], tm=512, tn=1024,
                                 gain=mla_kv_norm[layer], emit_h=True, out_dtype=BF16, name="mla_kv")
        kv_cache = jnp.concatenate(
            [_norm_matmul(ckv_cache_rows, 0, MLA_KV_LORA, w_ukv_p[layer], tm=past, tn=1024, out_dtype=BF16,
                          xrow0=b * depth + layer, m=past, name="mla_kv_cache") for b in range(nb_l)], axis=0)
        new_ckv.append(ckv_n[:m_c].reshape(nb_c, l_c, MLA_KV_LORA))
        kr_c = proj[:m_c, COL_KROPE:COL_KROPE + MLA_ROPE]
        new_kr.append(kr_c.reshape(nb_c, l_c, MLA_ROPE))
        y_mla_c = _mla_attention(
            (q, pl.BlockSpec((l_c, GROUP_W), lambda b, i: (b, 0))),
            (q, pl.BlockSpec((l_c, N_HEADS * MLA_ROPE), lambda b, i: (b, GROUP_W // (N_HEADS * MLA_ROPE)))),
            [((kv, pl.BlockSpec((l_c, GROUP_W), lambda b, i: (b, 0))),
              (proj, pl.BlockSpec((l_c, LANES), lambda b, i: (b, COL_KROPE // LANES))),
              (kv, pl.BlockSpec((l_c, GROUP_W), lambda b, i: (b, 1))))],
            nb=nb_c, lq=l_c, tq=l_c, name="mla_context")
        q_l = q[m_c:]
        qr_l = _axial_rope(q_l[:, GROUP_W:].reshape(nb_l, l_l, N_HEADS, MLA_ROPE).transpose(0, 2, 1, 3))
        qr_l = qr_l.transpose(0, 2, 1, 3).reshape(m_l, N_HEADS * MLA_ROPE)
        kr_l = _axial_rope(proj[m_c:, COL_KROPE:COL_KROPE + MLA_ROPE].reshape(nb_l, l_l, MLA_ROPE))
        kr_l = kr_l.reshape(m_l, MLA_ROPE)
        tq = 256
        nq = l_l // tq
        rb_l = m_c // l_l
        y_mla_l = _mla_attention(
            (q, pl.BlockSpec((tq, GROUP_W), lambda b, i: (m_c // tq + b * nq + i, 0))),
            (qr_l, pl.BlockSpec((tq, N_HEADS * MLA_ROPE), lambda b, i: (b * nq + i, 0))),
            [((kv_cache, pl.BlockSpec((past, GROUP_W), lambda b, i: (b, 0))),
              (cache_mla_krope, pl.BlockSpec((None, None, past, MLA_ROPE), lambda b, i: (b, layer, 0, 0))),
              (kv_cache, pl.BlockSpec((past, GROUP_W), lambda b, i: (b, 1)))),
             ((kv, pl.BlockSpec((l_l, GROUP_W), lambda b, i: (rb_l + b, 0))),
              (kr_l, pl.BlockSpec((l_l, MLA_ROPE), lambda b, i: (b, 0))),
              (kv, pl.BlockSpec((l_l, GROUP_W), lambda b, i: (rb_l + b, 1))))],
            nb=nb_l, lq=l_l, tq=tq, name="mla_latent")

        lg = lg_all[layer]
        y_ret_c, s_c = _retention(proj, 0, nb_c, l_c, l_c, lg, ret_gn[layer], emit_state=True)
        y_ret_l = _retention(proj, m_c, nb_l, l_l, 256, lg, ret_gn[layer], state=state_ret, layer=layer)
        new_s.append(s_c)

        yp = _out_matmul((y_hy_c, y_na_c, y_mla_c, y_ret_c), (y_hy_l, y_na_l, y_mla_l, y_ret_l), w_out_b[layer])
        x1, h_ffn, top_i, top_g, pos, cnt = _post_router(
            yp, x, norm_mix_post[layer], ga, norm_ffn_pre[layer], cf, sf, rw_hi[layer], rw_lo[layer],
            rb_p[layer:layer + 1], grp_fn(256))

        counts = cnt[0, :n_exp].astype(jnp.int32)
        y_rows, dest = _moe_experts(h_ffn, top_i[:, :TOP_K], pos[:, :TOP_K], counts,
                                    moe_w1, moe_b1, moe_w2, moe_b2, layer)
        dest = dest.reshape(m_c + m_l, TOP_K)
        ys = [y_rows.at[dest[:, kk]].get(mode="promise_in_bounds") for kk in range(TOP_K)]
        x = _combine(ys, top_g, x1, norm_ffn_post[layer], gf, grp_fn(128))

    return (x[:m_c].reshape(nb_c, l_c, d), x[m_c:].reshape(nb_l, l_l, d),
            jnp.stack(new_k, axis=1), jnp.stack(new_v, axis=1), jnp.stack(new_ckv, axis=1),
            jnp.stack(new_kr, axis=1), jnp.stack(new_s, axis=1))
```
